```python
import math
import jax, jax.numpy as jnp
from jax import lax
import numpy as np

D_MODEL = 1024
BATCH = 2
SEQ = 8192
DEPTH = 2
DEC_BATCH = 8
DEC_SEQ = 4096
PAST_LEN = 128

N_RET_HEADS = 4
RET_DK = D_MODEL // 8
RET_DV = 2 * RET_DK
RET_CHUNK = 128
N_DIFF_HEADS = D_MODEL // 128
DIFF_DH = 64
Q_BLOCK = 128
RET_W = N_RET_HEADS * RET_DV
DIFF_W = N_DIFF_HEADS * 2 * DIFF_DH
MIX_W = RET_W + DIFF_W
CONV_W = 2 * D_MODEL
CONV_K = 31
T5_BUCKETS = 32
T5_MAX_DIST = 128
ROPE_BASE = 10000.0
EPS = 1e-6

N_EVEN = (DEPTH + 1) // 2
N_ODD = DEPTH // 2

EVEN_SPLITS = [N_RET_HEADS * RET_DK, N_RET_HEADS * RET_DK, RET_W, RET_W,
               N_DIFF_HEADS * 2 * DIFF_DH, N_DIFF_HEADS * 2 * DIFF_DH, DIFF_W, DIFF_W]
EVEN_IN_W = int(sum(EVEN_SPLITS))
EVEN_SPLIT_IDX = [int(i) for i in np.cumsum(EVEN_SPLITS)[:-1]]

kernel_name = "hybrid_bidir_retention_diffattn_conformer"


def rms_norm(x, g):
    xf = x.astype(jnp.float32)
    y = xf * lax.rsqrt(jnp.mean(xf * xf, axis=-1, keepdims=True) + EPS)
    return (y * g.astype(jnp.float32)).astype(x.dtype)


def head_rms(x):
    xf = x.astype(jnp.float32)
    y = xf * lax.rsqrt(jnp.mean(xf * xf, axis=-1, keepdims=True) + EPS)
    return y.astype(x.dtype)


def layer_norm(x, g, b):
    xf = x.astype(jnp.float32)
    mu = jnp.mean(xf, axis=-1, keepdims=True)
    var = jnp.mean(jnp.square(xf - mu), axis=-1, keepdims=True)
    y = (xf - mu) * lax.rsqrt(var + EPS)
    return (y * g.astype(jnp.float32) + b.astype(jnp.float32)).astype(x.dtype)


def rope(x, pos):
    d = x.shape[-1]
    inv = 1.0 / (ROPE_BASE ** (jnp.arange(0, d, 2, dtype=jnp.float32) / d))
    ang = pos.astype(jnp.float32)[:, None] * inv[None, :]
    cos = jnp.cos(ang)[None, :, None, :]
    sin = jnp.sin(ang)[None, :, None, :]
    xf = x.astype(jnp.float32)
    x1, x2 = xf[..., : d // 2], xf[..., d // 2:]
    return jnp.concatenate([x1 * cos - x2 * sin, x1 * sin + x2 * cos], axis=-1).astype(x.dtype)


def t5_bucket(rel):
    nb = T5_BUCKETS // 2
    max_exact = nb // 2
    ret = (rel > 0).astype(jnp.int32) * nb
    n = jnp.abs(rel)
    nf = jnp.maximum(n, 1).astype(jnp.float32)
    large = max_exact + (jnp.log(nf / max_exact) / math.log(T5_MAX_DIST / max_exact)
                         * (nb - max_exact)).astype(jnp.int32)
    large = jnp.minimum(large, nb - 1)
    return ret + jnp.where(n < max_exact, n, large)


def bidir_retention(q, k, v, log_g):
    B, S, H, dk = q.shape
    dv = v.shape[-1]
    C = RET_CHUNK
    N = S // C
    dt = q.dtype
    lf, lb = log_g[0], log_g[1]
    qc = q.reshape(B, N, C, H, dk)
    kc = k.reshape(B, N, C, H, dk)
    vc = v.reshape(B, N, C, H, dv)
    pos = jnp.arange(C, dtype=jnp.float32)
    diff = pos[:, None] - pos[None, :]
    dmat = (jnp.where(diff >= 0, jnp.exp(lf[:, None, None] * jnp.maximum(diff, 0.0)), 0.0)
            + jnp.where(diff < 0, jnp.exp(lb[:, None, None] * jnp.maximum(-diff, 0.0)), 0.0))
    scores = jnp.einsum('bnqhd,bnkhd->bnhqk', qc, kc) * dmat.astype(dt)[None, None]
    intra = jnp.einsum('bnhqk,bnkhe->bnqhe', scores, vc)
    w_f = jnp.exp(lf[None, :] * (C - 1 - pos)[:, None]).astype(dt)
    w_b = jnp.exp(lb[None, :] * pos[:, None]).astype(dt)
    kv_f = jnp.einsum('bnkhd,kh,bnkhe->nbhde', kc, w_f, vc)
    kv_b = jnp.einsum('bnkhd,kh,bnkhe->nbhde', kc, w_b, vc)
    dec_f = jnp.exp(lf * C).astype(kv_f.dtype)[None, :, None, None]
    dec_b = jnp.exp(lb * C).astype(kv_b.dtype)[None, :, None, None]
    zero = jnp.zeros((B, H, dk, dv), kv_f.dtype)

    def step_f(R, kv):
        return dec_f * R + kv, R

    def step_b(L, kv):
        return dec_b * L + kv, L

    _, R_f = lax.scan(step_f, zero, kv_f)
    _, L_b = lax.scan(step_b, zero, kv_b, reverse=True)
    q_f = jnp.exp(lf[None, :] * (pos + 1.0)[:, None]).astype(dt)
    q_b = jnp.exp(lb[None, :] * (C - pos)[:, None]).astype(dt)
    inter = (jnp.einsum('bnqhd,qh,nbhde->bnqhe', qc, q_f, R_f)
             + jnp.einsum('bnqhd,qh,nbhde->bnqhe', qc, q_b, L_b))
    return (intra + inter).reshape(B, S, H, dv)


def diff_attention(q, k, v, lam, rel_bias):
    B, S, H, _, dh = q.shape
    nb = S // Q_BLOCK
    qb = q.reshape(B, nb, Q_BLOCK, H, 2, dh).transpose(1, 0, 2, 3, 4, 5)
    starts = jnp.arange(nb, dtype=jnp.int32) * Q_BLOCK
    kpos = jnp.arange(S, dtype=jnp.int32)
    scale = dh ** -0.5

    def block(args):
        qblk, s0 = args
        qpos = s0 + jnp.arange(Q_BLOCK, dtype=jnp.int32)
        bias = rel_bias[t5_bucket(kpos[None, :] - qpos[:, None])]
        bias = bias.transpose(2, 0, 1).astype(jnp.float32)
        s = jnp.einsum('bqhcd,bkhcd->bhcqk', qblk, k).astype(jnp.float32) * scale + bias[None, :, None]
        p = jax.nn.softmax(s, axis=-1)
        a = p[:, :, 0] - lam * p[:, :, 1]
        return jnp.einsum('bhqk,bkhe->bqhe', a.astype(v.dtype), v)

    o = lax.map(block, (qb, starts))
    return o.transpose(1, 0, 2, 3, 4).reshape(B, S, H, 2 * dh)


def even_layer(x, g, w_in, ret_decay, lq1, lk1, lq2, lk2, dnorm_g, w_out, rel_bias, layer_idx):
    B, S, _ = x.shape
    h = rms_norm(x, g)
    u = h @ w_in
    rq, rk, rv, rg, dq, dk, dv, dg = jnp.split(u, EVEN_SPLIT_IDX, axis=-1)
    pos = jnp.arange(S)
    rq = rope(rq.reshape(B, S, N_RET_HEADS, RET_DK), pos)
    rk = rope(rk.reshape(B, S, N_RET_HEADS, RET_DK), pos) * (RET_DK ** -0.5)
    rv = rv.reshape(B, S, N_RET_HEADS, RET_DV)
    log_g = -jnp.exp(ret_decay.astype(jnp.float32))
    ro = head_rms(bidir_retention(rq, rk, rv, log_g)).reshape(B, S, RET_W)
    ro = ro * jax.nn.silu(rg)
    lam_init = 0.8 - 0.6 * math.exp(-0.3 * layer_idx)
    lam = (jnp.exp(jnp.sum(lq1.astype(jnp.float32) * lk1.astype(jnp.float32)))
           - jnp.exp(jnp.sum(lq2.astype(jnp.float32) * lk2.astype(jnp.float32))) + lam_init)
    dq = dq.reshape(B, S, N_DIFF_HEADS, 2, DIFF_DH)
    dk = dk.reshape(B, S, N_DIFF_HEADS, 2, DIFF_DH)
    dv = dv.reshape(B, S, N_DIFF_HEADS, 2 * DIFF_DH)
    do = diff_attention(dq, dk, dv, lam, rel_bias)
    do = (rms_norm(do, dnorm_g) * (1.0 - lam_init)).reshape(B, S, DIFF_W)
    do = do * jax.nn.silu(dg)
    return x + jnp.concatenate([ro, do], axis=-1) @ w_out


def odd_layer(x, g, w_in, conv_w, conv_b, cn_g, cn_b, w_out):
    h = rms_norm(x, g)
    u = h @ w_in
    a, b, gate = jnp.split(u, [CONV_W, 2 * CONV_W], axis=-1)
    glu = a * jax.nn.sigmoid(b)
    y = lax.conv_general_dilated(glu, conv_w[:, None, :].astype(glu.dtype), window_strides=(1,),
                                 padding=((CONV_K // 2, CONV_K // 2),),
                                 dimension_numbers=('NWC', 'WIO', 'NWC'),
                                 feature_group_count=CONV_W) + conv_b
    y = jax.nn.silu(layer_norm(y, cn_g, cn_b))
    y = y * jax.nn.silu(gate)
    return x + y @ w_out


def run_trunk(x, norm_g, final_g, rel_bias, w_in_mix, ret_decay, lam_q1, lam_k1, lam_q2, lam_k2,
              diff_norm_g, w_out_mix, w_in_conv, conv_w, conv_b, conv_norm_g, conv_norm_b, w_out_conv):
    for l in range(DEPTH):
        if l % 2 == 0:
            e = l // 2
            x = even_layer(x, norm_g[l], w_in_mix[e], ret_decay[e], lam_q1[e], lam_k1[e], lam_q2[e],
                           lam_k2[e], diff_norm_g[e], w_out_mix[e], rel_bias, l)
        else:
            o = l // 2
            x = odd_layer(x, norm_g[l], w_in_conv[o], conv_w[o], conv_b[o], conv_norm_g[o],
                          conv_norm_b[o], w_out_conv[o])
    return rms_norm(x, final_g)


def setup_inputs(seed: int = 0) -> dict:
    key = jax.random.key(seed)
    ks = jax.random.split(key, 20)
    f32 = jnp.float32
    nrm = lambda k, s, sc: jax.random.normal(k, s, f32) * sc
    base_decay = jnp.log(-jnp.log(1.0 - 2.0 ** (-5.0 - jnp.arange(N_RET_HEADS, dtype=f32))))
    return {
        "x_prompt": nrm(ks[0], (BATCH, SEQ, D_MODEL), 1.0),
        "x_sample": nrm(ks[1], (DEC_BATCH, DEC_SEQ, D_MODEL), 1.0),
        "norm_g": 1.0 + nrm(ks[2], (DEPTH, D_MODEL), 0.02),
        "final_g": 1.0 + nrm(ks[3], (D_MODEL,), 0.02),
        "rel_bias": nrm(ks[4], (T5_BUCKETS, N_DIFF_HEADS), 0.5),
        "w_in_mix": nrm(ks[5], (N_EVEN, D_MODEL, EVEN_IN_W), D_MODEL ** -0.5),
        "ret_decay": base_decay[None, None, :] + nrm(ks[6], (N_EVEN, 2, N_RET_HEADS), 0.05),
        "lam_q1": nrm(ks[7], (N_EVEN, DIFF_DH), 0.1),
        "lam_k1": nrm(ks[8], (N_EVEN, DIFF_DH), 0.1),
        "lam_q2": nrm(ks[9], (N_EVEN, DIFF_DH), 0.1),
        "lam_k2": nrm(ks[10], (N_EVEN, DIFF_DH), 0.1),
        "diff_norm_g": 1.0 + nrm(ks[11], (N_EVEN, 2 * DIFF_DH), 0.02),
        "w_out_mix": nrm(ks[12], (N_EVEN, MIX_W, D_MODEL), MIX_W ** -0.5),
        "w_in_conv": nrm(ks[13], (N_ODD, D_MODEL, 3 * CONV_W), D_MODEL ** -0.5),
        "conv_w": nrm(ks[14], (N_ODD, CONV_K, CONV_W), CONV_K ** -0.5),
        "conv_b": nrm(ks[15], (N_ODD, CONV_W), 0.02),
        "conv_norm_g": 1.0 + nrm(ks[16], (N_ODD, CONV_W), 0.02),
        "conv_norm_b": nrm(ks[17], (N_ODD, CONV_W), 0.02),
        "w_out_conv": nrm(ks[18], (N_ODD, CONV_W, D_MODEL), CONV_W ** -0.5),
    }


def reference(x_prompt, x_sample, norm_g, final_g, rel_bias, w_in_mix, ret_decay, lam_q1, lam_k1,
              lam_q2, lam_k2, diff_norm_g, w_out_mix, w_in_conv, conv_w, conv_b, conv_norm_g,
              conv_norm_b, w_out_conv):
    y_prompt = run_trunk(x_prompt, norm_g, final_g, rel_bias, w_in_mix, ret_decay, lam_q1, lam_k1,
                         lam_q2, lam_k2, diff_norm_g, w_out_mix, w_in_conv, conv_w, conv_b,
                         conv_norm_g, conv_norm_b, w_out_conv)
    y_sample = run_trunk(x_sample, norm_g, final_g, rel_bias, w_in_mix, ret_decay, lam_q1, lam_k1,
                         lam_q2, lam_k2, diff_norm_g, w_out_mix, w_in_conv, conv_w, conv_b,
                         conv_norm_g, conv_norm_b, w_out_conv)
    return (y_prompt, y_sample)
```

```python
import functools
import math

import numpy as np
import jax
import jax.numpy as jnp
from jax import lax
from jax.experimental import pallas as pl
from jax.experimental.pallas import tpu as pltpu

F32 = jnp.float32
BF16 = jnp.bfloat16

D_MODEL = 1024
N_RET_HEADS = 4
RET_DK = 128
RET_DV = 256
N_DIFF_HEADS = 8
DIFF_DH = 64
DIFF_DV = 2 * DIFF_DH
RET_W = N_RET_HEADS * RET_DV
DIFF_W = N_DIFF_HEADS * DIFF_DV
CONV_W = 2 * D_MODEL
CONV_K = 31
CONV_PAD = CONV_K // 2
T5_BUCKETS = 32
T5_MAX_DIST = 128
ROPE_BASE = 10000.0
EPS = 1e-6

_C_RQ = 0
_C_RK = _C_RQ + N_RET_HEADS * RET_DK
_C_RV = _C_RK + N_RET_HEADS * RET_DK
_C_RG = _C_RV + RET_W
_C_DQ = _C_RG + RET_W
_C_DK = _C_DQ + DIFF_W
_C_DV = _C_DK + DIFF_W
_C_DG = _C_DV + DIFF_W
EVEN_IN_W = _C_DG + DIFF_W

LANES = 128
HALO = 16
VMEM_LIMIT = 56 * 1024 * 1024

ROW_TILE = 256
RET_CHUNK = 256
ATT_TILE = 512
CONV_TILE = 256
CONV_ROWS = 64

M_INIT = -1e30


def _rms(x, g):
    return x * lax.rsqrt(jnp.mean(x * x, axis=-1, keepdims=True) + EPS) * g


def _silu(x):
    return x * jax.nn.sigmoid(x)


def _params(sem):
    return pltpu.CompilerParams(dimension_semantics=sem, vmem_limit_bytes=VMEM_LIMIT)


def _whole(shape):
    nd = len(shape)
    return pl.BlockSpec(shape, lambda *_: (0,) * nd, pipeline_mode=pl.Buffered(1))


def _even_inproj_kernel(x_ref, g_ref, w_ref, wvt_ref, cos_ref, sin_ref,
                        rq_ref, rk_ref, rv_ref, rg_ref, dq_ref, dk_ref, dvt_ref, dg_ref):
    hb = _rms(x_ref[...], g_ref[...]).astype(BF16)

    def proj(c0, n):
        return jnp.dot(hb, w_ref[:, c0:c0 + n], preferred_element_type=F32)

    cos = cos_ref[...]
    sin = sin_ref[...]

    def rope(u):
        return u * cos + pltpu.roll(u, RET_DK // 2, 1) * sin

    uq = proj(_C_RQ, N_RET_HEADS * RET_DK)
    uk = proj(_C_RK, N_RET_HEADS * RET_DK)
    for j in range(N_RET_HEADS):
        sl = slice(j * RET_DK, (j + 1) * RET_DK)
        rq_ref[:, sl] = rope(uq[:, sl]).astype(BF16)
        rk_ref[:, sl] = (rope(uk[:, sl]) * (RET_DK ** -0.5)).astype(BF16)
    rv_ref[...] = proj(_C_RV, RET_W).astype(BF16)
    rg_ref[...] = proj(_C_RG, RET_W)
    dq_ref[...] = (proj(_C_DQ, DIFF_W) * (DIFF_DH ** -0.5)).astype(BF16)
    dk_ref[...] = proj(_C_DK, DIFF_W).astype(BF16)
    dg_ref[...] = proj(_C_DG, DIFF_W)
    dvt_ref[...] = lax.dot_general(wvt_ref[...], hb, (((1,), (1,)), ((), ())),
                                   preferred_element_type=F32).astype(BF16)


def _even_inproj(x, g, w, wvt, cos, sin):
    B, S, D = x.shape
    tm = ROW_TILE
    row = lambda n: pl.BlockSpec((None, tm, n), lambda b, i: (b, i, 0))
    tab = pl.BlockSpec((tm, LANES), lambda b, i: (i, 0))
    out_shape = (
        jax.ShapeDtypeStruct((B, S, N_RET_HEADS * RET_DK), BF16),
        jax.ShapeDtypeStruct((B, S, N_RET_HEADS * RET_DK), BF16),
        jax.ShapeDtypeStruct((B, S, RET_W), BF16),
        jax.ShapeDtypeStruct((B, S, RET_W), F32),
        jax.ShapeDtypeStruct((B, S, DIFF_W), BF16),
        jax.ShapeDtypeStruct((B, S, DIFF_W), BF16),
        jax.ShapeDtypeStruct((B, DIFF_W, S), BF16),
        jax.ShapeDtypeStruct((B, S, DIFF_W), F32),
    )
    out_specs = (row(N_RET_HEADS * RET_DK), row(N_RET_HEADS * RET_DK), row(RET_W), row(RET_W),
                 row(DIFF_W), row(DIFF_W),
                 pl.BlockSpec((None, DIFF_W, tm), lambda b, i: (b, 0, i)),
                 row(DIFF_W))
    return pl.pallas_call(
        _even_inproj_kernel,
        grid=(B, S // tm),
        in_specs=[row(D), _whole((1, D)), _whole(w.shape), _whole(wvt.shape), tab, tab],
        out_specs=out_specs,
        out_shape=out_shape,
        compiler_params=_params(("parallel", "parallel")),
        name="even_inproj",
    )(x, g, w, wvt, cos, sin)


def _retention_kernel(lg_ref, q_ref, k_ref, v_ref, g_ref, o_ref, oacc_ref, st_ref, dmat_ref):
    h = pl.program_id(1)
    n = pl.program_id(2)
    nchunks = pl.num_programs(2) // 2
    C = RET_CHUNK
    lf = lg_ref[h]
    lb = lg_ref[N_RET_HEADS + h]
    pos = lax.broadcasted_iota(jnp.int32, (C, 1), 0).astype(F32)

    q = q_ref[...]
    k = k_ref[...]
    v = v_ref[...]

    def state_update(decay_c, w_col):
        kw = (k.astype(F32) * w_col).T.astype(BF16)
        st_ref[...] = decay_c * st_ref[...] + jnp.dot(kw, v, preferred_element_type=F32)

    def inter(q_col):
        qs = (q.astype(F32) * q_col).astype(BF16)
        return jnp.dot(qs, st_ref[...].astype(BF16), preferred_element_type=F32)

    @pl.when(n == 0)
    def _():
        r = lax.broadcasted_iota(jnp.int32, (C, C), 0)
        c = lax.broadcasted_iota(jnp.int32, (C, C), 1)
        d = (r - c).astype(F32)
        dmat_ref[...] = jnp.where(d >= 0, jnp.exp(lf * jnp.maximum(d, 0.0)),
                                  jnp.exp(lb * jnp.maximum(-d, 0.0)))

    @pl.when((n == 0) | (n == nchunks))
    def _():
        st_ref[...] = jnp.zeros_like(st_ref)

    @pl.when(n < nchunks)
    def _():
        s = lax.dot_general(q, k, (((1,), (1,)), ((), ())), preferred_element_type=F32)
        s = (s * dmat_ref[...]).astype(BF16)
        intra = jnp.dot(s, v, preferred_element_type=F32)
        row0 = pl.multiple_of(n * C, C)
        oacc_ref[pl.ds(row0, C), :] = intra + inter(jnp.exp(lf * (pos + 1.0)))
        state_update(jnp.exp(jnp.full((1, RET_DV), lf * C, F32)), jnp.exp(lf * (C - 1.0 - pos)))

    @pl.when(n >= nchunks)
    def _():
        row0 = pl.multiple_of((2 * nchunks - 1 - n) * C, C)
        o = oacc_ref[pl.ds(row0, C), :] + inter(jnp.exp(lb * (C - pos)))
        o = o * lax.rsqrt(jnp.mean(o * o, axis=-1, keepdims=True) + EPS)
        o_ref[...] = (o * _silu(g_ref[...])).astype(BF16)
        state_update(jnp.exp(jnp.full((1, RET_DV), lb * C, F32)), jnp.exp(lb * pos))


def _retention(log_g, rq, rk, rv, rg):
    B, S, _ = rq.shape
    C = RET_CHUNK
    N = S // C

    def chunk(n):
        return jnp.where(n < N, n, 2 * N - 1 - n)

    def late(n):
        return jnp.where(n < N, N - 1, 2 * N - 1 - n)

    qk_spec = pl.BlockSpec((None, C, RET_DK), lambda b, h, n, lg: (b, chunk(n), h))
    v_spec = pl.BlockSpec((None, C, RET_DV), lambda b, h, n, lg: (b, chunk(n), h))
    late_spec = pl.BlockSpec((None, C, RET_DV), lambda b, h, n, lg: (b, late(n), h))
    grid_spec = pltpu.PrefetchScalarGridSpec(
        num_scalar_prefetch=1,
        grid=(B, N_RET_HEADS, 2 * N),
        in_specs=[qk_spec, qk_spec, v_spec, late_spec],
        out_specs=late_spec,
        scratch_shapes=[pltpu.VMEM((S, RET_DV), F32),
                        pltpu.VMEM((RET_DK, RET_DV), F32),
                        pltpu.VMEM((C, C), F32)],
    )
    return pl.pallas_call(
        _retention_kernel,
        grid_spec=grid_spec,
        out_shape=jax.ShapeDtypeStruct((B, S, RET_W), BF16),
        compiler_params=_params(("parallel", "parallel", "arbitrary")),
        name="retention",
    )(log_g, rq, rk, rv, rg)


def _diff_attn_kernel(sc_ref, q_ref, k_ref, vt_ref, bias_ref, g_ref, ng_ref, o_ref,
                      qz1_ref, qz2_ref, m_ref, l_ref, acc_ref, *, lam_init):
    h = pl.program_id(1)
    qi = pl.program_id(2)
    ki = pl.program_id(3)
    nk = pl.num_programs(3)
    lam = sc_ref[0]
    bias_left = sc_ref[1 + h]
    bias_right = sc_ref[1 + N_DIFF_HEADS + h]

    @pl.when(ki == 0)
    def _():
        q = q_ref[...]
        lane = lax.broadcasted_iota(jnp.int32, q.shape, 1)
        zero = jnp.zeros_like(q)
        qz1_ref[...] = jnp.where(lane < DIFF_DH, q, zero)
        qz2_ref[...] = jnp.where(lane >= DIFF_DH, q, zero)
        m_ref[...] = jnp.full_like(m_ref, M_INIT)
        l_ref[...] = jnp.zeros_like(l_ref)
        acc_ref[...] = jnp.zeros_like(acc_ref)

    @pl.when(ki == qi - 1)
    def _():
        m_ref[...] = m_ref[...] + bias_left

    @pl.when(ki == qi + 2)
    def _():
        m_ref[...] = m_ref[...] - bias_right

    k = k_ref[...]
    vt = vt_ref[...]

    def step(c, qz_ref, with_bias):
        s = lax.dot_general(k, qz_ref[...], (((1,), (1,)), ((), ())), preferred_element_type=F32)
        if with_bias:
            s = s + bias_ref[...]
        m_old = m_ref[c]
        m_new = jnp.maximum(m_old, jnp.max(s, axis=0, keepdims=True))
        alpha = jnp.exp(m_old - m_new)
        p = jnp.exp(s - m_new)
        l_ref[c] = alpha * l_ref[c] + jnp.sum(p, axis=0, keepdims=True)
        acc_ref[c] = alpha * acc_ref[c] + jnp.dot(vt, p.astype(BF16), preferred_element_type=F32)
        m_ref[c] = m_new

    near = (ki >= qi - 1) & (ki <= qi + 1)

    @pl.when(near)
    def _():
        step(0, qz1_ref, True)
        step(1, qz2_ref, True)

    @pl.when(jnp.logical_not(near))
    def _():
        step(0, qz1_ref, False)
        step(1, qz2_ref, False)

    @pl.when(ki == nk - 1)
    def _():
        ot = acc_ref[0] / l_ref[0] - lam * (acc_ref[1] / l_ref[1])
        o = ot.T
        o = _rms(o, ng_ref[...]) * (1.0 - lam_init)
        o_ref[...] = (o * _silu(g_ref[...])).astype(BF16)


def _diff_attn(scal, dq, dk, dvt, bias_t, dg, ng, lam_init):
    B, S, _ = dq.shape
    T = ATT_TILE
    nt = S // T
    q_spec = pl.BlockSpec((None, T, DIFF_DV), lambda b, h, qi, ki, sc: (b, qi, h))
    k_spec = pl.BlockSpec((None, T, DIFF_DV), lambda b, h, qi, ki, sc: (b, ki, h))
    vt_spec = pl.BlockSpec((None, DIFF_DV, T), lambda b, h, qi, ki, sc: (b, h, ki))
    bias_spec = pl.BlockSpec((None, None, T, T),
                             lambda b, h, qi, ki, sc: (h, jnp.clip(ki - qi + 1, 0, 2), 0, 0))
    ng_spec = pl.BlockSpec((1, DIFF_DV), lambda b, h, qi, ki, sc: (0, 0))
    grid_spec = pltpu.PrefetchScalarGridSpec(
        num_scalar_prefetch=1,
        grid=(B, N_DIFF_HEADS, nt, nt),
        in_specs=[q_spec, k_spec, vt_spec, bias_spec, q_spec, ng_spec],
        out_specs=q_spec,
        scratch_shapes=[pltpu.VMEM((T, DIFF_DV), BF16),
                        pltpu.VMEM((T, DIFF_DV), BF16),
                        pltpu.VMEM((2, 1, T), F32),
                        pltpu.VMEM((2, 1, T), F32),
                        pltpu.VMEM((2, DIFF_DV, T), F32)],
    )
    return pl.pallas_call(
        functools.partial(_diff_attn_kernel, lam_init=lam_init),
        grid_spec=grid_spec,
        out_shape=jax.ShapeDtypeStruct((B, S, DIFF_W), BF16),
        compiler_params=_params(("parallel", "parallel", "parallel", "arbitrary")),
        name="diff_attn",
    )(scal, dq, dk, dvt, bias_t, dg, ng)


def _mid_proj_kernel(x_ref, ro_ref, do_ref, wo_ref, g_ref, wi_ref, x1_ref, glu_ref, sg_ref):
    x1 = (x_ref[...]
          + jnp.dot(ro_ref[...], wo_ref[:RET_W, :], preferred_element_type=F32)
          + jnp.dot(do_ref[...], wo_ref[RET_W:, :], preferred_element_type=F32))
    x1_ref[...] = x1
    hb = _rms(x1, g_ref[...]).astype(BF16)
    step = 512
    for c0 in range(0, CONV_W, step):
        a = jnp.dot(hb, wi_ref[:, c0:c0 + step], preferred_element_type=F32)
        b = jnp.dot(hb, wi_ref[:, CONV_W + c0:CONV_W + c0 + step], preferred_element_type=F32)
        gate = jnp.dot(hb, wi_ref[:, 2 * CONV_W + c0:2 * CONV_W + c0 + step], preferred_element_type=F32)
        glu_ref[:, c0:c0 + step] = a * jax.nn.sigmoid(b)
        sg_ref[:, c0:c0 + step] = _silu(gate)


def _mid_proj(x, ro, do, wo, g, wi):
    B, S, D = x.shape
    tm = ROW_TILE
    row = lambda n: pl.BlockSpec((None, tm, n), lambda b, i: (b, i, 0))
    return pl.pallas_call(
        _mid_proj_kernel,
        grid=(B, S // tm),
        in_specs=[row(D), row(RET_W), row(DIFF_W), _whole(wo.shape), _whole((1, D)), _whole(wi.shape)],
        out_specs=(row(D), row(CONV_W), row(CONV_W)),
        out_shape=(jax.ShapeDtypeStruct((B, S, D), F32),
                   jax.ShapeDtypeStruct((B, S, CONV_W), F32),
                   jax.ShapeDtypeStruct((B, S, CONV_W), F32)),
        compiler_params=_params(("parallel", "parallel")),
        name="mid_proj",
    )(x, ro, do, wo, g, wi)


def _conv_out_kernel(glu_ref, prev_ref, next_ref, sg_ref, x1_ref, cw_ref, cb_ref, ng_ref, nb_ref,
                     wo_ref, fg_ref, o_ref, win_ref, y_ref):
    i = pl.program_id(1)
    nblk = pl.num_programs(1)
    TS = CONV_TILE
    ncb = CONV_W // LANES
    prev = jnp.where(i > 0, prev_ref[...], 0.0)
    nxt = jnp.where(i < nblk - 1, next_ref[...], 0.0)
    for cb in range(ncb):
        sl = slice(cb * LANES, (cb + 1) * LANES)
        win_ref[cb, 0:HALO, :] = prev[:, sl]
        win_ref[cb, HALO:HALO + TS, :] = glu_ref[:, sl]
        win_ref[cb, HALO + TS:HALO + TS + HALO, :] = nxt[:, sl]

    def chan_block(cb, carry):
        w = cw_ref[cb]
        for r0 in range(0, TS, CONV_ROWS):
            acc = jnp.zeros((CONV_ROWS, LANES), F32)
            for t in range(CONV_K):
                off = r0 + HALO - CONV_PAD + t
                acc = acc + win_ref[cb, pl.ds(off, CONV_ROWS), :] * w[t:t + 1, :]
            y_ref[cb, r0:r0 + CONV_ROWS, :] = acc
        return carry

    lax.fori_loop(0, ncb, chan_block, 0)

    y = jnp.concatenate([y_ref[cb] for cb in range(ncb)], axis=-1) + cb_ref[...]
    mu = jnp.mean(y, axis=-1, keepdims=True)
    yc = y - mu
    var = jnp.mean(yc * yc, axis=-1, keepdims=True)
    yn = yc * lax.rsqrt(var + EPS) * ng_ref[...] + nb_ref[...]
    z = (_silu(yn) * sg_ref[...]).astype(BF16)
    out = x1_ref[...] + jnp.dot(z, wo_ref[...], preferred_element_type=F32)
    o_ref[...] = _rms(out, fg_ref[...])


def _conv_out(glu, sg, x1, cw, cb, ng, nb, wo, fg):
    B, S, D = x1.shape
    TS = CONV_TILE
    hb = TS // HALO
    nh = S // HALO
    row = lambda n: pl.BlockSpec((None, TS, n), lambda b, i: (b, i, 0))
    prev_spec = pl.BlockSpec((None, HALO, CONV_W), lambda b, i: (b, jnp.maximum(i * hb - 1, 0), 0))
    next_spec = pl.BlockSpec((None, HALO, CONV_W), lambda b, i: (b, jnp.minimum((i + 1) * hb, nh - 1), 0))
    ncb = CONV_W // LANES
    return pl.pallas_call(
        _conv_out_kernel,
        grid=(B, S // TS),
        in_specs=[row(CONV_W), prev_spec, next_spec, row(CONV_W), row(D),
                  _whole(cw.shape), _whole((1, CONV_W)), _whole((1, CONV_W)), _whole((1, CONV_W)),
                  _whole(wo.shape), _whole((1, D))],
        out_specs=row(D),
        out_shape=jax.ShapeDtypeStruct((B, S, D), F32),
        scratch_shapes=[pltpu.VMEM((ncb, TS + 2 * HALO, LANES), F32),
                        pltpu.VMEM((ncb, TS, LANES), F32)],
        compiler_params=_params(("parallel", "parallel")),
        name="conv_out",
    )(glu, glu, glu, sg, x1, cw, cb, ng, nb, wo, fg)


def _t5_bucket_np(rel):
    nb = T5_BUCKETS // 2
    max_exact = nb // 2
    ret = (rel > 0).astype(np.int64) * nb
    n = np.abs(rel)
    nf = np.maximum(n, 1).astype(np.float64)
    large = max_exact + (np.log(nf / max_exact) / math.log(T5_MAX_DIST / max_exact)
                         * (nb - max_exact)).astype(np.int64)
    large = np.minimum(large, nb - 1)
    return ret + np.where(n < max_exact, n, large)


def _rope_tables(S):
    half = RET_DK // 2
    inv = 1.0 / (ROPE_BASE ** (jnp.arange(0, RET_DK, 2, dtype=F32) / RET_DK))
    ang = jnp.arange(S).astype(F32)[:, None] * inv[None, :]
    cos = jnp.cos(ang)
    sin = jnp.sin(ang)
    assert cos.shape == (S, half)
    return jnp.concatenate([cos, cos], axis=-1), jnp.concatenate([-sin, sin], axis=-1)


def _trunk(x, p):
    B, S, D = x.shape
    assert D == D_MODEL and S % ATT_TILE == 0 and S % RET_CHUNK == 0 and S % ROW_TILE == 0

    cos, sin = _rope_tables(S)
    rq, rk, rv, rg, dq, dk, dvt, dg = _even_inproj(x, p["g0"], p["w_in_mix"], p["wvt"], cos, sin)
    ro = _retention(p["log_g"], rq, rk, rv, rg)
    do = _diff_attn(p["attn_scal"], dq, dk, dvt, p["bias_t"], dg, p["diff_norm_g"], p["lam_init"])
    x1, glu, sg = _mid_proj(x, ro, do, p["w_out_mix"], p["g1"], p["w_in_conv"])
    return _conv_out(glu, sg, x1, p["conv_w"], p["conv_b"], p["conv_norm_g"], p["conv_norm_b"],
                     p["w_out_conv"], p["final_g"])


def kernel(x_prompt, x_sample, norm_g, final_g, rel_bias, w_in_mix, ret_decay, lam_q1, lam_k1, lam_q2,
           lam_k2, diff_norm_g, w_out_mix, w_in_conv, conv_w, conv_b, conv_norm_g, conv_norm_b,
           w_out_conv):
    assert norm_g.shape[0] == 2 and w_in_mix.shape[0] == 1 and w_in_conv.shape[0] == 1
    lam_init = 0.8 - 0.6 * math.exp(-0.3 * 0)
    lam = (jnp.exp(jnp.sum(lam_q1[0].astype(F32) * lam_k1[0].astype(F32)))
           - jnp.exp(jnp.sum(lam_q2[0].astype(F32) * lam_k2[0].astype(F32))) + lam_init)
    T = ATT_TILE
    kk = np.arange(T)[:, None]
    qq = np.arange(T)[None, :]
    idx = np.stack([_t5_bucket_np(kk - qq + d * T) for d in (-1, 0, 1)]).astype(np.int32)
    bias_t = jnp.take(rel_bias.astype(F32).T, jnp.asarray(idx), axis=1)
    far_left = int(_t5_bucket_np(np.array(-T5_MAX_DIST)))
    far_right = int(_t5_bucket_np(np.array(T5_MAX_DIST)))
    attn_scal = jnp.concatenate([lam.reshape(1), rel_bias[far_left].astype(F32),
                                 rel_bias[far_right].astype(F32)])
    wim = w_in_mix[0]
    ncb = CONV_W // LANES
    p = dict(
        g0=norm_g[0].reshape(1, D_MODEL), g1=norm_g[1].reshape(1, D_MODEL),
        final_g=final_g.reshape(1, D_MODEL),
        w_in_mix=wim.astype(BF16),
        wvt=wim[:, _C_DV:_C_DV + DIFF_W].T.astype(BF16),
        log_g=(-jnp.exp(ret_decay[0].astype(F32))).reshape(-1),
        attn_scal=attn_scal, bias_t=bias_t, lam_init=lam_init,
        diff_norm_g=diff_norm_g[0].reshape(1, DIFF_DV),
        w_out_mix=w_out_mix[0].astype(BF16),
        w_in_conv=w_in_conv[0].astype(BF16),
        conv_w=conv_w[0].reshape(CONV_K, ncb, LANES).transpose(1, 0, 2),
        conv_b=conv_b[0].reshape(1, CONV_W),
        conv_norm_g=conv_norm_g[0].reshape(1, CONV_W),
        conv_norm_b=conv_norm_b[0].reshape(1, CONV_W),
        w_out_conv=w_out_conv[0].astype(BF16),
    )
    return (_trunk(x_prompt, p), _trunk(x_sample, p))
```

```python
import functools
import math

import numpy as np
import jax
import jax.numpy as jnp
from jax import lax
from jax.experimental import pallas as pl
from jax.experimental.pallas import tpu as pltpu

F32 = jnp.float32
BF16 = jnp.bfloat16

D_MODEL = 1024
N_RET_HEADS = 4
RET_DK = 128
RET_DV = 256
N_DIFF_HEADS = 8
DIFF_DH = 64
DIFF_DV = 2 * DIFF_DH
RET_W = N_RET_HEADS * RET_DV
DIFF_W = N_DIFF_HEADS * DIFF_DV
CONV_W = 2 * D_MODEL
CONV_K = 31
CONV_PAD = CONV_K // 2
T5_BUCKETS = 32
T5_MAX_DIST = 128
ROPE_BASE = 10000.0
EPS = 1e-6

_C_RQ = 0
_C_RK = _C_RQ + N_RET_HEADS * RET_DK
_C_RV = _C_RK + N_RET_HEADS * RET_DK
_C_RG = _C_RV + RET_W
_C_DQ = _C_RG + RET_W
_C_DK = _C_DQ + DIFF_W
_C_DV = _C_DK + DIFF_W
_C_DG = _C_DV + DIFF_W
EVEN_IN_W = _C_DG + DIFF_W

LANES = 128
HALO = 16
VMEM_LIMIT = 56 * 1024 * 1024

ROW_TILE = 256
RET_CHUNK = 256
ATT_TILE = 512
CONV_TILE = 256
CONV_ROWS = 64

M_INIT = -1e30


def _rms(x, g):
    return x * lax.rsqrt(jnp.mean(x * x, axis=-1, keepdims=True) + EPS) * g


def _silu(x):
    return x * jax.nn.sigmoid(x)


def _params(sem):
    return pltpu.CompilerParams(dimension_semantics=sem, vmem_limit_bytes=VMEM_LIMIT)


def _whole(shape):
    nd = len(shape)
    return pl.BlockSpec(shape, lambda *_: (0,) * nd, pipeline_mode=pl.Buffered(1))


def _even_inproj_kernel(x_ref, g_ref, w_ref, wvt_ref, cos_ref, sin_ref,
                        rq_ref, rk_ref, rv_ref, rg_ref, dq_ref, dk_ref, dvt_ref, dg_ref):
    hb = _rms(x_ref[...], g_ref[...]).astype(BF16)

    def proj(c0, n):
        return jnp.dot(hb, w_ref[:, c0:c0 + n], preferred_element_type=F32)

    cos = cos_ref[...]
    sin = sin_ref[...]

    def rope(u):
        return u * cos + pltpu.roll(u, RET_DK // 2, 1) * sin

    uq = proj(_C_RQ, N_RET_HEADS * RET_DK)
    uk = proj(_C_RK, N_RET_HEADS * RET_DK)
    for j in range(N_RET_HEADS):
        sl = slice(j * RET_DK, (j + 1) * RET_DK)
        rq_ref[:, sl] = rope(uq[:, sl]).astype(BF16)
        rk_ref[:, sl] = (rope(uk[:, sl]) * (RET_DK ** -0.5)).astype(BF16)
    rv_ref[...] = proj(_C_RV, RET_W).astype(BF16)
    rg_ref[...] = proj(_C_RG, RET_W)
    dq_ref[...] = (proj(_C_DQ, DIFF_W) * (DIFF_DH ** -0.5)).astype(BF16)
    dk_ref[...] = proj(_C_DK, DIFF_W).astype(BF16)
    dg_ref[...] = proj(_C_DG, DIFF_W)
    dvt_ref[...] = lax.dot_general(wvt_ref[...], hb, (((1,), (1,)), ((), ())),
                                   preferred_element_type=F32).astype(BF16)


def _even_inproj(x, g, w, wvt, cos, sin):
    B, S, D = x.shape
    tm = ROW_TILE
    row = lambda n: pl.BlockSpec((None, tm, n), lambda b, i: (b, i, 0))
    tab = pl.BlockSpec((tm, LANES), lambda b, i: (i, 0))
    out_shape = (
        jax.ShapeDtypeStruct((B, S, N_RET_HEADS * RET_DK), BF16),
        jax.ShapeDtypeStruct((B, S, N_RET_HEADS * RET_DK), BF16),
        jax.ShapeDtypeStruct((B, S, RET_W), BF16),
        jax.ShapeDtypeStruct((B, S, RET_W), F32),
        jax.ShapeDtypeStruct((B, S, DIFF_W), BF16),
        jax.ShapeDtypeStruct((B, S, DIFF_W), BF16),
        jax.ShapeDtypeStruct((B, S // ATT_TILE, DIFF_W, ATT_TILE), BF16),
        jax.ShapeDtypeStruct((B, S, DIFF_W), F32),
    )
    per_tile = ATT_TILE // tm
    out_specs = (row(N_RET_HEADS * RET_DK), row(N_RET_HEADS * RET_DK), row(RET_W), row(RET_W),
                 row(DIFF_W), row(DIFF_W),
                 pl.BlockSpec((None, None, DIFF_W, tm), lambda b, i: (b, i // per_tile, 0, i % per_tile)),
                 row(DIFF_W))
    return pl.pallas_call(
        _even_inproj_kernel,
        grid=(B, S // tm),
        in_specs=[row(D), _whole((1, D)), _whole(w.shape), _whole(wvt.shape), tab, tab],
        out_specs=out_specs,
        out_shape=out_shape,
        compiler_params=_params(("parallel", "parallel")),
        name="even_inproj",
    )(x, g, w, wvt, cos, sin)


def _retention_kernel(lg_ref, q_ref, k_ref, v_ref, g_ref, o_ref, oacc_ref, st_ref, dmat_ref):
    h = pl.program_id(1)
    n = pl.program_id(2)
    nchunks = pl.num_programs(2) // 2
    C = RET_CHUNK
    lf = lg_ref[h]
    lb = lg_ref[N_RET_HEADS + h]
    pos = lax.broadcasted_iota(jnp.int32, (C, 1), 0).astype(F32)

    q = q_ref[...]
    k = k_ref[...]
    v = v_ref[...]

    def state_update(decay_c, w_col):
        kw = (k.astype(F32) * w_col).T.astype(BF16)
        st_ref[...] = decay_c * st_ref[...] + jnp.dot(kw, v, preferred_element_type=F32)

    def inter(q_col):
        qs = (q.astype(F32) * q_col).astype(BF16)
        return jnp.dot(qs, st_ref[...].astype(BF16), preferred_element_type=F32)

    @pl.when(n == 0)
    def _():
        r = lax.broadcasted_iota(jnp.int32, (C, C), 0)
        c = lax.broadcasted_iota(jnp.int32, (C, C), 1)
        d = (r - c).astype(F32)
        dmat_ref[...] = jnp.where(d >= 0, jnp.exp(lf * jnp.maximum(d, 0.0)),
                                  jnp.exp(lb * jnp.maximum(-d, 0.0)))

    @pl.when((n == 0) | (n == nchunks))
    def _():
        st_ref[...] = jnp.zeros_like(st_ref)

    @pl.when(n < nchunks)
    def _():
        s = lax.dot_general(q, k, (((1,), (1,)), ((), ())), preferred_element_type=F32)
        s = (s * dmat_ref[...]).astype(BF16)
        intra = jnp.dot(s, v, preferred_element_type=F32)
        row0 = pl.multiple_of(n * C, C)
        oacc_ref[pl.ds(row0, C), :] = intra + inter(jnp.exp(lf * (pos + 1.0)))
        state_update(jnp.exp(jnp.full((1, RET_DV), lf * C, F32)), jnp.exp(lf * (C - 1.0 - pos)))

    @pl.when(n >= nchunks)
    def _():
        row0 = pl.multiple_of((2 * nchunks - 1 - n) * C, C)
        o = oacc_ref[pl.ds(row0, C), :] + inter(jnp.exp(lb * (C - pos)))
        o = o * lax.rsqrt(jnp.mean(o * o, axis=-1, keepdims=True) + EPS)
        o_ref[...] = (o * _silu(g_ref[...])).astype(BF16)
        state_update(jnp.exp(jnp.full((1, RET_DV), lb * C, F32)), jnp.exp(lb * pos))


def _retention(log_g, rq, rk, rv, rg):
    B, S, _ = rq.shape
    C = RET_CHUNK
    N = S // C

    def chunk(n):
        return jnp.where(n < N, n, 2 * N - 1 - n)

    def late(n):
        return jnp.where(n < N, N - 1, 2 * N - 1 - n)

    qk_spec = pl.BlockSpec((None, C, RET_DK), lambda b, h, n, lg: (b, chunk(n), h))
    v_spec = pl.BlockSpec((None, C, RET_DV), lambda b, h, n, lg: (b, chunk(n), h))
    late_spec = pl.BlockSpec((None, C, RET_DV), lambda b, h, n, lg: (b, late(n), h))
    grid_spec = pltpu.PrefetchScalarGridSpec(
        num_scalar_prefetch=1,
        grid=(B, N_RET_HEADS, 2 * N),
        in_specs=[qk_spec, qk_spec, v_spec, late_spec],
        out_specs=late_spec,
        scratch_shapes=[pltpu.VMEM((S, RET_DV), F32),
                        pltpu.VMEM((RET_DK, RET_DV), F32),
                        pltpu.VMEM((C, C), F32)],
    )
    return pl.pallas_call(
        _retention_kernel,
        grid_spec=grid_spec,
        out_shape=jax.ShapeDtypeStruct((B, S, RET_W), BF16),
        compiler_params=_params(("parallel", "parallel", "arbitrary")),
        name="retention",
    )(log_g, rq, rk, rv, rg)


def _diff_attn_kernel(sc_ref, q_ref, k_ref, vt_ref, btab_ref, g_ref, ng_ref, o_ref,
                      qz_ref, bias_ref, m_ref, l_ref, acc_ref, *, lam_init):
    h = pl.program_id(1)
    qi = pl.program_id(2)
    T = ATT_TILE
    nk = vt_ref.shape[0]
    lam = sc_ref[0]
    bias_left = sc_ref[1 + h]
    bias_right = sc_ref[1 + N_DIFF_HEADS + h]

    @pl.when(qi == 0)
    def _():
        for d in range(3):
            x = jnp.broadcast_to(btab_ref[d], (T, 2 * T))
            bias_ref[d] = pltpu.roll(x, 0, 1, stride=1, stride_axis=0)[:, :T]

    q = q_ref[...]
    lane = lax.broadcasted_iota(jnp.int32, q.shape, 1)
    zero = jnp.zeros_like(q)
    qz_ref[0] = jnp.where(lane < DIFF_DH, q, zero)
    qz_ref[1] = jnp.where(lane >= DIFF_DH, q, zero)
    m_ref[...] = jnp.full_like(m_ref, M_INIT)
    l_ref[...] = jnp.zeros_like(l_ref)
    acc_ref[...] = jnp.zeros_like(acc_ref)

    def tile(j, bias):
        k = k_ref[pl.ds(pl.multiple_of(j * T, T), T), :]
        vt = vt_ref[j]
        for c in range(2):
            s = lax.dot_general(k, qz_ref[c], (((1,), (1,)), ((), ())), preferred_element_type=F32)
            if bias is not None:
                s = s + bias
            m_old = m_ref[c]
            m_new = jnp.maximum(m_old, jnp.max(s, axis=0, keepdims=True))
            alpha = jnp.exp(m_old - m_new)
            p = jnp.exp(s - m_new)
            l_ref[c] = alpha * l_ref[c] + jnp.sum(p, axis=0, keepdims=True)
            acc_ref[c] = alpha * acc_ref[c] + jnp.dot(vt, p.astype(BF16), preferred_element_type=F32)
            m_ref[c] = m_new

    def far_tile(j, carry):
        tile(j, None)
        return carry

    lax.fori_loop(0, qi - 1, far_tile, 0)
    m_ref[...] = m_ref[...] + bias_left
    for d in range(3):
        j = qi + d - 1

        @pl.when((j >= 0) & (j < nk))
        def _():
            tile(jnp.clip(j, 0, nk - 1), bias_ref[d])

    m_ref[...] = m_ref[...] - bias_right
    lax.fori_loop(qi + 2, nk, far_tile, 0)

    ot = acc_ref[0] / l_ref[0] - lam * (acc_ref[1] / l_ref[1])
    o = _rms(ot.T, ng_ref[...]) * (1.0 - lam_init)
    o_ref[...] = (o * _silu(g_ref[...])).astype(BF16)


def _diff_attn(scal, dq, dk, dvt, btab, dg, ng, lam_init):
    B, S, _ = dq.shape
    T = ATT_TILE
    nt = S // T
    q_spec = pl.BlockSpec((None, T, DIFF_DV), lambda b, h, qi, sc: (b, qi, h))
    k_spec = pl.BlockSpec((None, S, DIFF_DV), lambda b, h, qi, sc: (b, 0, h))
    vt_spec = pl.BlockSpec((None, nt, DIFF_DV, T), lambda b, h, qi, sc: (b, 0, h, 0))
    btab_spec = pl.BlockSpec((None, 3, 1, 2 * T), lambda b, h, qi, sc: (h, 0, 0, 0))
    ng_spec = pl.BlockSpec((1, DIFF_DV), lambda b, h, qi, sc: (0, 0))
    grid_spec = pltpu.PrefetchScalarGridSpec(
        num_scalar_prefetch=1,
        grid=(B, N_DIFF_HEADS, nt),
        in_specs=[q_spec, k_spec, vt_spec, btab_spec, q_spec, ng_spec],
        out_specs=q_spec,
        scratch_shapes=[pltpu.VMEM((2, T, DIFF_DV), BF16),
                        pltpu.VMEM((3, T, T), F32),
                        pltpu.VMEM((2, 1, T), F32),
                        pltpu.VMEM((2, 1, T), F32),
                        pltpu.VMEM((2, DIFF_DV, T), F32)],
    )
    return pl.pallas_call(
        functools.partial(_diff_attn_kernel, lam_init=lam_init),
        grid_spec=grid_spec,
        out_shape=jax.ShapeDtypeStruct((B, S, DIFF_W), BF16),
        compiler_params=_params(("parallel", "parallel", "arbitrary")),
        name="diff_attn",
    )(scal, dq, dk, dvt, btab, dg, ng)


def _mid_proj_kernel(x_ref, ro_ref, do_ref, wo_ref, g_ref, wi_ref, x1_ref, glu_ref, sg_ref):
    x1 = (x_ref[...]
          + jnp.dot(ro_ref[...], wo_ref[:RET_W, :], preferred_element_type=F32)
          + jnp.dot(do_ref[...], wo_ref[RET_W:, :], preferred_element_type=F32))
    x1_ref[...] = x1
    hb = _rms(x1, g_ref[...]).astype(BF16)
    step = 512
    for c0 in range(0, CONV_W, step):
        a = jnp.dot(hb, wi_ref[:, c0:c0 + step], preferred_element_type=F32)
        b = jnp.dot(hb, wi_ref[:, CONV_W + c0:CONV_W + c0 + step], preferred_element_type=F32)
        gate = jnp.dot(hb, wi_ref[:, 2 * CONV_W + c0:2 * CONV_W + c0 + step], preferred_element_type=F32)
        glu_ref[:, c0:c0 + step] = a * jax.nn.sigmoid(b)
        sg_ref[:, c0:c0 + step] = _silu(gate)


def _mid_proj(x, ro, do, wo, g, wi):
    B, S, D = x.shape
    tm = ROW_TILE
    row = lambda n: pl.BlockSpec((None, tm, n), lambda b, i: (b, i, 0))
    return pl.pallas_call(
        _mid_proj_kernel,
        grid=(B, S // tm),
        in_specs=[row(D), row(RET_W), row(DIFF_W), _whole(wo.shape), _whole((1, D)), _whole(wi.shape)],
        out_specs=(row(D), row(CONV_W), row(CONV_W)),
        out_shape=(jax.ShapeDtypeStruct((B, S, D), F32),
                   jax.ShapeDtypeStruct((B, S, CONV_W), F32),
                   jax.ShapeDtypeStruct((B, S, CONV_W), F32)),
        compiler_params=_params(("parallel", "parallel")),
        name="mid_proj",
    )(x, ro, do, wo, g, wi)


def _conv_out_kernel(glu_ref, prev_ref, next_ref, sg_ref, x1_ref, cw_ref, cb_ref, ng_ref, nb_ref,
                     wo_ref, fg_ref, o_ref, win_ref, y_ref):
    i = pl.program_id(1)
    nblk = pl.num_programs(1)
    TS = CONV_TILE
    ncb = CONV_W // LANES
    prev = jnp.where(i > 0, prev_ref[...], 0.0)
    nxt = jnp.where(i < nblk - 1, next_ref[...], 0.0)
    for cb in range(ncb):
        sl = slice(cb * LANES, (cb + 1) * LANES)
        win_ref[cb, 0:HALO, :] = prev[:, sl]
        win_ref[cb, HALO:HALO + TS, :] = glu_ref[:, sl]
        win_ref[cb, HALO + TS:HALO + TS + HALO, :] = nxt[:, sl]

    def chan_block(cb, carry):
        w = cw_ref[cb]
        for r0 in range(0, TS, CONV_ROWS):
            acc = jnp.zeros((CONV_ROWS, LANES), F32)
            for t in range(CONV_K):
                off = r0 + HALO - CONV_PAD + t
                acc = acc + win_ref[cb, pl.ds(off, CONV_ROWS), :] * w[t:t + 1, :]
            y_ref[cb, r0:r0 + CONV_ROWS, :] = acc
        return carry

    lax.fori_loop(0, ncb, chan_block, 0)

    y = jnp.concatenate([y_ref[cb] for cb in range(ncb)], axis=-1) + cb_ref[...]
    mu = jnp.mean(y, axis=-1, keepdims=True)
    yc = y - mu
    var = jnp.mean(yc * yc, axis=-1, keepdims=True)
    yn = yc * lax.rsqrt(var + EPS) * ng_ref[...] + nb_ref[...]
    z = (_silu(yn) * sg_ref[...]).astype(BF16)
    out = x1_ref[...] + jnp.dot(z, wo_ref[...], preferred_element_type=F32)
    o_ref[...] = _rms(out, fg_ref[...])


def _conv_out(glu, sg, x1, cw, cb, ng, nb, wo, fg):
    B, S, D = x1.shape
    TS = CONV_TILE
    hb = TS // HALO
    nh = S // HALO
    row = lambda n: pl.BlockSpec((None, TS, n), lambda b, i: (b, i, 0))
    prev_spec = pl.BlockSpec((None, HALO, CONV_W), lambda b, i: (b, jnp.maximum(i * hb - 1, 0), 0))
    next_spec = pl.BlockSpec((None, HALO, CONV_W), lambda b, i: (b, jnp.minimum((i + 1) * hb, nh - 1), 0))
    ncb = CONV_W // LANES
    return pl.pallas_call(
        _conv_out_kernel,
        grid=(B, S // TS),
        in_specs=[row(CONV_W), prev_spec, next_spec, row(CONV_W), row(D),
                  _whole(cw.shape), _whole((1, CONV_W)), _whole((1, CONV_W)), _whole((1, CONV_W)),
                  _whole(wo.shape), _whole((1, D))],
        out_specs=row(D),
        out_shape=jax.ShapeDtypeStruct((B, S, D), F32),
        scratch_shapes=[pltpu.VMEM((ncb, TS + 2 * HALO, LANES), F32),
                        pltpu.VMEM((ncb, TS, LANES), F32)],
        compiler_params=_params(("parallel", "parallel")),
        name="conv_out",
    )(glu, glu, glu, sg, x1, cw, cb, ng, nb, wo, fg)


def _t5_bucket_np(rel):
    nb = T5_BUCKETS // 2
    max_exact = nb // 2
    ret = (rel > 0).astype(np.int64) * nb
    n = np.abs(rel)
    nf = np.maximum(n, 1).astype(np.float64)
    large = max_exact + (np.log(nf / max_exact) / math.log(T5_MAX_DIST / max_exact)
                         * (nb - max_exact)).astype(np.int64)
    large = np.minimum(large, nb - 1)
    return ret + np.where(n < max_exact, n, large)


def _rope_tables(S):
    half = RET_DK // 2
    inv = 1.0 / (ROPE_BASE ** (jnp.arange(0, RET_DK, 2, dtype=F32) / RET_DK))
    ang = jnp.arange(S).astype(F32)[:, None] * inv[None, :]
    cos = jnp.cos(ang)
    sin = jnp.sin(ang)
    assert cos.shape == (S, half)
    return jnp.concatenate([cos, cos], axis=-1), jnp.concatenate([-sin, sin], axis=-1)


def _trunk(x, p):
    B, S, D = x.shape
    assert D == D_MODEL and S % ATT_TILE == 0 and S % RET_CHUNK == 0 and S % ROW_TILE == 0

    cos, sin = _rope_tables(S)
    rq, rk, rv, rg, dq, dk, dvt, dg = _even_inproj(x, p["g0"], p["w_in_mix"], p["wvt"], cos, sin)
    ro = _retention(p["log_g"], rq, rk, rv, rg)
    do = _diff_attn(p["attn_scal"], dq, dk, dvt, p["btab"], dg, p["diff_norm_g"], p["lam_init"])
    x1, glu, sg = _mid_proj(x, ro, do, p["w_out_mix"], p["g1"], p["w_in_conv"])
    return _conv_out(glu, sg, x1, p["conv_w"], p["conv_b"], p["conv_norm_g"], p["conv_norm_b"],
                     p["w_out_conv"], p["final_g"])


def kernel(x_prompt, x_sample, norm_g, final_g, rel_bias, w_in_mix, ret_decay, lam_q1, lam_k1, lam_q2,
           lam_k2, diff_norm_g, w_out_mix, w_in_conv, conv_w, conv_b, conv_norm_g, conv_norm_b,
           w_out_conv):
    assert norm_g.shape[0] == 2 and w_in_mix.shape[0] == 1 and w_in_conv.shape[0] == 1
    lam_init = 0.8 - 0.6 * math.exp(-0.3 * 0)
    lam = (jnp.exp(jnp.sum(lam_q1[0].astype(F32) * lam_k1[0].astype(F32)))
           - jnp.exp(jnp.sum(lam_q2[0].astype(F32) * lam_k2[0].astype(F32))) + lam_init)
    T = ATT_TILE
    j = np.arange(2 * T)
    idx = np.stack([_t5_bucket_np(np.where(j <= T, d * T - j, d * T + 2 * T - j))
                    for d in (-1, 0, 1)]).astype(np.int32)
    btab = jnp.take(rel_bias.astype(F32).T, jnp.asarray(idx), axis=1).reshape(
        N_DIFF_HEADS, 3, 1, 2 * T)
    far_left = int(_t5_bucket_np(np.array(-T5_MAX_DIST)))
    far_right = int(_t5_bucket_np(np.array(T5_MAX_DIST)))
    attn_scal = jnp.concatenate([lam.reshape(1), rel_bias[far_left].astype(F32),
                                 rel_bias[far_right].astype(F32)])
    wim = w_in_mix[0]
    ncb = CONV_W // LANES
    p = dict(
        g0=norm_g[0].reshape(1, D_MODEL), g1=norm_g[1].reshape(1, D_MODEL),
        final_g=final_g.reshape(1, D_MODEL),
        w_in_mix=wim.astype(BF16),
        wvt=wim[:, _C_DV:_C_DV + DIFF_W].T.astype(BF16),
        log_g=(-jnp.exp(ret_decay[0].astype(F32))).reshape(-1),
        attn_scal=attn_scal, btab=btab, lam_init=lam_init,
        diff_norm_g=diff_norm_g[0].reshape(1, DIFF_DV),
        w_out_mix=w_out_mix[0].astype(BF16),
        w_in_conv=w_in_conv[0].astype(BF16),
        conv_w=conv_w[0].reshape(CONV_K, ncb, LANES).transpose(1, 0, 2),
        conv_b=conv_b[0].reshape(1, CONV_W),
        conv_norm_g=conv_norm_g[0].reshape(1, CONV_W),
        conv_norm_b=conv_norm_b[0].reshape(1, CONV_W),
        w_out_conv=w_out_conv[0].astype(BF16),
    )
    return (_trunk(x_prompt, p), _trunk(x_sample, p))
```

```python
import functools
import math

import numpy as np
import jax
import jax.numpy as jnp
from jax import lax
from jax.experimental import pallas as pl
from jax.experimental.pallas import tpu as pltpu

F32 = jnp.float32
BF16 = jnp.bfloat16

D_MODEL = 1024
N_RET_HEADS = 4
RET_DK = 128
RET_DV = 256
N_DIFF_HEADS = 8
DIFF_DH = 64
DIFF_DV = 2 * DIFF_DH
RET_W = N_RET_HEADS * RET_DV
DIFF_W = N_DIFF_HEADS * DIFF_DV
CONV_W = 2 * D_MODEL
CONV_K = 31
CONV_PAD = CONV_K // 2
T5_BUCKETS = 32
T5_MAX_DIST = 128
ROPE_BASE = 10000.0
EPS = 1e-6

_C_RQ = 0
_C_RK = _C_RQ + N_RET_HEADS * RET_DK
_C_RV = _C_RK + N_RET_HEADS * RET_DK
_C_RG = _C_RV + RET_W
_C_DQ = _C_RG + RET_W
_C_DK = _C_DQ + DIFF_W
_C_DV = _C_DK + DIFF_W
_C_DG = _C_DV + DIFF_W
EVEN_IN_W = _C_DG + DIFF_W

LANES = 128
HALO = 16
VMEM_LIMIT = 56 * 1024 * 1024

ROW_TILE = 256
RET_CHUNK = 256
ATT_TILE = 512
ONES_ROWS = 16
LOG2E = math.log2(math.e)
CONV_TILE = 256
CONV_ROWS = 64

M_INIT = -1e30


def _rms(x, g):
    return x * lax.rsqrt(jnp.mean(x * x, axis=-1, keepdims=True) + EPS) * g


def _silu(x):
    return x * jax.nn.sigmoid(x)


def _params(sem):
    return pltpu.CompilerParams(dimension_semantics=sem, vmem_limit_bytes=VMEM_LIMIT)


def _whole(shape):
    nd = len(shape)
    return pl.BlockSpec(shape, lambda *_: (0,) * nd, pipeline_mode=pl.Buffered(1))


def _even_inproj_kernel(x_ref, g_ref, w_ref, wvt_ref, cos_ref, sin_ref,
                        rq_ref, rk_ref, rv_ref, rg_ref, dq_ref, dk_ref, dvt_ref, dg_ref):
    hb = _rms(x_ref[...], g_ref[...]).astype(BF16)

    def proj(c0, n):
        return jnp.dot(hb, w_ref[:, c0:c0 + n], preferred_element_type=F32)

    cos = cos_ref[...]
    sin = sin_ref[...]

    def rope(u):
        return u * cos + pltpu.roll(u, RET_DK // 2, 1) * sin

    uq = proj(_C_RQ, N_RET_HEADS * RET_DK)
    uk = proj(_C_RK, N_RET_HEADS * RET_DK)
    for j in range(N_RET_HEADS):
        sl = slice(j * RET_DK, (j + 1) * RET_DK)
        rq_ref[:, sl] = rope(uq[:, sl]).astype(BF16)
        rk_ref[:, sl] = (rope(uk[:, sl]) * (RET_DK ** -0.5)).astype(BF16)
    rv_ref[...] = proj(_C_RV, RET_W).astype(BF16)
    rg_ref[...] = proj(_C_RG, RET_W)
    dq_ref[...] = (proj(_C_DQ, DIFF_W) * (DIFF_DH ** -0.5 * LOG2E)).astype(BF16)
    dk_ref[...] = proj(_C_DK, DIFF_W).astype(BF16)
    dg_ref[...] = proj(_C_DG, DIFF_W)
    dvt_ref[...] = lax.dot_general(wvt_ref[...], hb, (((1,), (1,)), ((), ())),
                                   preferred_element_type=F32).astype(BF16)


def _even_inproj(x, g, w, wvt, cos, sin):
    B, S, D = x.shape
    tm = ROW_TILE
    row = lambda n: pl.BlockSpec((None, tm, n), lambda b, i: (b, i, 0))
    tab = pl.BlockSpec((tm, LANES), lambda b, i: (i, 0))
    out_shape = (
        jax.ShapeDtypeStruct((B, S, N_RET_HEADS * RET_DK), BF16),
        jax.ShapeDtypeStruct((B, S, N_RET_HEADS * RET_DK), BF16),
        jax.ShapeDtypeStruct((B, S, RET_W), BF16),
        jax.ShapeDtypeStruct((B, S, RET_W), F32),
        jax.ShapeDtypeStruct((B, S, DIFF_W), BF16),
        jax.ShapeDtypeStruct((B, S, DIFF_W), BF16),
        jax.ShapeDtypeStruct((B, S // ATT_TILE, DIFF_W, ATT_TILE), BF16),
        jax.ShapeDtypeStruct((B, S, DIFF_W), F32),
    )
    per_tile = ATT_TILE // tm
    out_specs = (row(N_RET_HEADS * RET_DK), row(N_RET_HEADS * RET_DK), row(RET_W), row(RET_W),
                 row(DIFF_W), row(DIFF_W),
                 pl.BlockSpec((None, None, DIFF_W, tm), lambda b, i: (b, i // per_tile, 0, i % per_tile)),
                 row(DIFF_W))
    return pl.pallas_call(
        _even_inproj_kernel,
        grid=(B, S // tm),
        in_specs=[row(D), _whole((1, D)), _whole(w.shape), _whole(wvt.shape), tab, tab],
        out_specs=out_specs,
        out_shape=out_shape,
        compiler_params=_params(("parallel", "parallel")),
        name="even_inproj",
    )(x, g, w, wvt, cos, sin)


def _retention_kernel(lg_ref, q_ref, k_ref, v_ref, g_ref, o_ref, oacc_ref, st_ref, dmat_ref):
    h = pl.program_id(1)
    n = pl.program_id(2)
    nchunks = pl.num_programs(2) // 2
    C = RET_CHUNK
    lf = lg_ref[h]
    lb = lg_ref[N_RET_HEADS + h]
    pos = lax.broadcasted_iota(jnp.int32, (C, 1), 0).astype(F32)

    q = q_ref[...]
    k = k_ref[...]
    v = v_ref[...]

    def state_update(decay_c, w_col):
        kw = (k.astype(F32) * w_col).T.astype(BF16)
        st_ref[...] = decay_c * st_ref[...] + jnp.dot(kw, v, preferred_element_type=F32)

    def inter(q_col):
        qs = (q.astype(F32) * q_col).astype(BF16)
        return jnp.dot(qs, st_ref[...].astype(BF16), preferred_element_type=F32)

    @pl.when(n == 0)
    def _():
        r = lax.broadcasted_iota(jnp.int32, (C, C), 0)
        c = lax.broadcasted_iota(jnp.int32, (C, C), 1)
        d = (r - c).astype(F32)
        dmat_ref[...] = jnp.where(d >= 0, jnp.exp(lf * jnp.maximum(d, 0.0)),
                                  jnp.exp(lb * jnp.maximum(-d, 0.0)))

    @pl.when((n == 0) | (n == nchunks))
    def _():
        st_ref[...] = jnp.zeros_like(st_ref)

    @pl.when(n < nchunks)
    def _():
        s = lax.dot_general(q, k, (((1,), (1,)), ((), ())), preferred_element_type=F32)
        s = (s * dmat_ref[...]).astype(BF16)
        intra = jnp.dot(s, v, preferred_element_type=F32)
        row0 = pl.multiple_of(n * C, C)
        oacc_ref[pl.ds(row0, C), :] = intra + inter(jnp.exp(lf * (pos + 1.0)))
        state_update(jnp.exp(jnp.full((1, RET_DV), lf * C, F32)), jnp.exp(lf * (C - 1.0 - pos)))

    @pl.when(n >= nchunks)
    def _():
        row0 = pl.multiple_of((2 * nchunks - 1 - n) * C, C)
        o = oacc_ref[pl.ds(row0, C), :] + inter(jnp.exp(lb * (C - pos)))
        o = o * lax.rsqrt(jnp.mean(o * o, axis=-1, keepdims=True) + EPS)
        o_ref[...] = (o * _silu(g_ref[...])).astype(BF16)
        state_update(jnp.exp(jnp.full((1, RET_DV), lb * C, F32)), jnp.exp(lb * pos))


def _retention(log_g, rq, rk, rv, rg):
    B, S, _ = rq.shape
    C = RET_CHUNK
    N = S // C

    def chunk(n):
        return jnp.where(n < N, n, 2 * N - 1 - n)

    def late(n):
        return jnp.where(n < N, N - 1, 2 * N - 1 - n)

    qk_spec = pl.BlockSpec((None, C, RET_DK), lambda b, h, n, lg: (b, chunk(n), h))
    v_spec = pl.BlockSpec((None, C, RET_DV), lambda b, h, n, lg: (b, chunk(n), h))
    late_spec = pl.BlockSpec((None, C, RET_DV), lambda b, h, n, lg: (b, late(n), h))
    grid_spec = pltpu.PrefetchScalarGridSpec(
        num_scalar_prefetch=1,
        grid=(B, N_RET_HEADS, 2 * N),
        in_specs=[qk_spec, qk_spec, v_spec, late_spec],
        out_specs=late_spec,
        scratch_shapes=[pltpu.VMEM((S, RET_DV), F32),
                        pltpu.VMEM((RET_DK, RET_DV), F32),
                        pltpu.VMEM((C, C), F32)],
    )
    return pl.pallas_call(
        _retention_kernel,
        grid_spec=grid_spec,
        out_shape=jax.ShapeDtypeStruct((B, S, RET_W), BF16),
        compiler_params=_params(("parallel", "parallel", "arbitrary")),
        name="retention",
    )(log_g, rq, rk, rv, rg)


def _diff_attn_kernel(sc_ref, q_ref, k_ref, vt_ref, btab_ref, g_ref, ng_ref, o_ref,
                      qz_ref, bias_ref, m_ref, acc_ref, s0_ref, *, lam_init):
    h = pl.program_id(1)
    qi = pl.program_id(2)
    T = ATT_TILE
    nk = vt_ref.shape[0]
    lam = sc_ref[0]
    bias_left = sc_ref[1 + h]
    bias_right = sc_ref[1 + N_DIFF_HEADS + h]

    @pl.when(qi == 0)
    def _():
        for d in range(3):
            x = jnp.broadcast_to(btab_ref[d], (T, 2 * T))
            bias_ref[d] = pltpu.roll(x, 0, 1, stride=1, stride_axis=0)[:, :T]

    q = q_ref[...]
    lane = lax.broadcasted_iota(jnp.int32, q.shape, 1)
    zero = jnp.zeros_like(q)
    qz_ref[0] = jnp.where(lane < DIFF_DH, q, zero)
    qz_ref[1] = jnp.where(lane >= DIFF_DH, q, zero)
    m_ref[...] = jnp.full_like(m_ref, M_INIT)
    acc_ref[...] = jnp.zeros_like(acc_ref)
    ones = jnp.ones((ONES_ROWS, T), BF16)

    def scores(c, j):
        k = k_ref[pl.ds(pl.multiple_of(j * T, T), T), :]
        return lax.dot_general(k, qz_ref[c], (((1,), (1,)), ((), ())),
                               preferred_element_type=F32)

    def softmax_pv(c, vt, s, d):
        if d is not None:
            s = s + bias_ref[d]
        m_old = m_ref[c]
        m_new = jnp.maximum(m_old, jnp.max(s, axis=0, keepdims=True))
        p = jnp.exp2(s - m_new).astype(BF16)
        acc_ref[c] = jnp.exp2(m_old - m_new) * acc_ref[c] + jnp.dot(vt, p, preferred_element_type=F32)
        m_ref[c] = m_new

    def tile(j, d):
        vt = jnp.concatenate([vt_ref[j], ones], axis=0)
        s1 = scores(1, j)
        softmax_pv(0, vt, s0_ref[...], d)
        s0_ref[...] = scores(0, jnp.minimum(j + 1, nk - 1))
        softmax_pv(1, vt, s1, d)

    def far_tile(j, carry):
        tile(j, None)
        return carry

    s0_ref[...] = scores(0, 0)
    lax.fori_loop(0, qi - 1, far_tile, 0)
    m_ref[...] = m_ref[...] + bias_left
    for d in range(3):
        j = qi + d - 1

        @pl.when((j >= 0) & (j < nk))
        def _():
            tile(jnp.clip(j, 0, nk - 1), d)

    m_ref[...] = m_ref[...] - bias_right
    lax.fori_loop(qi + 2, nk, far_tile, 0)

    def normalised(c):
        return acc_ref[c, :DIFF_DV, :] / acc_ref[c, DIFF_DV:DIFF_DV + 1, :]

    ot = normalised(0) - lam * normalised(1)
    o = _rms(ot.T, ng_ref[...]) * (1.0 - lam_init)
    o_ref[...] = (o * _silu(g_ref[...])).astype(BF16)


def _diff_attn(scal, dq, dk, dvt, btab, dg, ng, lam_init):
    B, S, _ = dq.shape
    T = ATT_TILE
    nt = S // T
    q_spec = pl.BlockSpec((None, T, DIFF_DV), lambda b, h, qi, sc: (b, qi, h))
    k_spec = pl.BlockSpec((None, S, DIFF_DV), lambda b, h, qi, sc: (b, 0, h))
    vt_spec = pl.BlockSpec((None, nt, DIFF_DV, T), lambda b, h, qi, sc: (b, 0, h, 0))
    btab_spec = pl.BlockSpec((None, 3, 1, 2 * T), lambda b, h, qi, sc: (h, 0, 0, 0))
    ng_spec = pl.BlockSpec((1, DIFF_DV), lambda b, h, qi, sc: (0, 0))
    grid_spec = pltpu.PrefetchScalarGridSpec(
        num_scalar_prefetch=1,
        grid=(B, N_DIFF_HEADS, nt),
        in_specs=[q_spec, k_spec, vt_spec, btab_spec, q_spec, ng_spec],
        out_specs=q_spec,
        scratch_shapes=[pltpu.VMEM((2, T, DIFF_DV), BF16),
                        pltpu.VMEM((3, T, T), F32),
                        pltpu.VMEM((2, 1, T), F32),
                        pltpu.VMEM((2, DIFF_DV + ONES_ROWS, T), F32),
                        pltpu.VMEM((T, T), F32)],
    )
    return pl.pallas_call(
        functools.partial(_diff_attn_kernel, lam_init=lam_init),
        grid_spec=grid_spec,
        out_shape=jax.ShapeDtypeStruct((B, S, DIFF_W), BF16),
        compiler_params=_params(("parallel", "parallel", "arbitrary")),
        name="diff_attn",
    )(scal, dq, dk, dvt, btab, dg, ng)


def _mid_proj_kernel(x_ref, ro_ref, do_ref, wo_ref, g_ref, wi_ref, x1_ref, glu_ref, sg_ref):
    x1 = (x_ref[...]
          + jnp.dot(ro_ref[...], wo_ref[:RET_W, :], preferred_element_type=F32)
          + jnp.dot(do_ref[...], wo_ref[RET_W:, :], preferred_element_type=F32))
    x1_ref[...] = x1
    hb = _rms(x1, g_ref[...]).astype(BF16)
    step = 512
    for c0 in range(0, CONV_W, step):
        a = jnp.dot(hb, wi_ref[:, c0:c0 + step], preferred_element_type=F32)
        b = jnp.dot(hb, wi_ref[:, CONV_W + c0:CONV_W + c0 + step], preferred_element_type=F32)
        gate = jnp.dot(hb, wi_ref[:, 2 * CONV_W + c0:2 * CONV_W + c0 + step], preferred_element_type=F32)
        glu_ref[:, c0:c0 + step] = a * jax.nn.sigmoid(b)
        sg_ref[:, c0:c0 + step] = _silu(gate)


def _mid_proj(x, ro, do, wo, g, wi):
    B, S, D = x.shape
    tm = ROW_TILE
    row = lambda n: pl.BlockSpec((None, tm, n), lambda b, i: (b, i, 0))
    return pl.pallas_call(
        _mid_proj_kernel,
        grid=(B, S // tm),
        in_specs=[row(D), row(RET_W), row(DIFF_W), _whole(wo.shape), _whole((1, D)), _whole(wi.shape)],
        out_specs=(row(D), row(CONV_W), row(CONV_W)),
        out_shape=(jax.ShapeDtypeStruct((B, S, D), F32),
                   jax.ShapeDtypeStruct((B, S, CONV_W), F32),
                   jax.ShapeDtypeStruct((B, S, CONV_W), F32)),
        compiler_params=_params(("parallel", "parallel")),
        name="mid_proj",
    )(x, ro, do, wo, g, wi)


def _conv_out_kernel(glu_ref, prev_ref, next_ref, sg_ref, x1_ref, cw_ref, cb_ref, ng_ref, nb_ref,
                     wo_ref, fg_ref, o_ref, win_ref, y_ref):
    i = pl.program_id(1)
    nblk = pl.num_programs(1)
    TS = CONV_TILE
    ncb = CONV_W // LANES
    prev = jnp.where(i > 0, prev_ref[...], 0.0)
    nxt = jnp.where(i < nblk - 1, next_ref[...], 0.0)
    for cb in range(ncb):
        sl = slice(cb * LANES, (cb + 1) * LANES)
        win_ref[cb, 0:HALO, :] = prev[:, sl]
        win_ref[cb, HALO:HALO + TS, :] = glu_ref[:, sl]
        win_ref[cb, HALO + TS:HALO + TS + HALO, :] = nxt[:, sl]

    def chan_block(cb, carry):
        w = cw_ref[cb]
        for r0 in range(0, TS, CONV_ROWS):
            acc = jnp.zeros((CONV_ROWS, LANES), F32)
            for t in range(CONV_K):
                off = r0 + HALO - CONV_PAD + t
                acc = acc + win_ref[cb, pl.ds(off, CONV_ROWS), :] * w[t:t + 1, :]
            y_ref[cb, r0:r0 + CONV_ROWS, :] = acc
        return carry

    lax.fori_loop(0, ncb, chan_block, 0)

    y = jnp.concatenate([y_ref[cb] for cb in range(ncb)], axis=-1) + cb_ref[...]
    mu = jnp.mean(y, axis=-1, keepdims=True)
    yc = y - mu
    var = jnp.mean(yc * yc, axis=-1, keepdims=True)
    yn = yc * lax.rsqrt(var + EPS) * ng_ref[...] + nb_ref[...]
    z = (_silu(yn) * sg_ref[...]).astype(BF16)
    out = x1_ref[...] + jnp.dot(z, wo_ref[...], preferred_element_type=F32)
    o_ref[...] = _rms(out, fg_ref[...])


def _conv_out(glu, sg, x1, cw, cb, ng, nb, wo, fg):
    B, S, D = x1.shape
    TS = CONV_TILE
    hb = TS // HALO
    nh = S // HALO
    row = lambda n: pl.BlockSpec((None, TS, n), lambda b, i: (b, i, 0))
    prev_spec = pl.BlockSpec((None, HALO, CONV_W), lambda b, i: (b, jnp.maximum(i * hb - 1, 0), 0))
    next_spec = pl.BlockSpec((None, HALO, CONV_W), lambda b, i: (b, jnp.minimum((i + 1) * hb, nh - 1), 0))
    ncb = CONV_W // LANES
    return pl.pallas_call(
        _conv_out_kernel,
        grid=(B, S // TS),
        in_specs=[row(CONV_W), prev_spec, next_spec, row(CONV_W), row(D),
                  _whole(cw.shape), _whole((1, CONV_W)), _whole((1, CONV_W)), _whole((1, CONV_W)),
                  _whole(wo.shape), _whole((1, D))],
        out_specs=row(D),
        out_shape=jax.ShapeDtypeStruct((B, S, D), F32),
        scratch_shapes=[pltpu.VMEM((ncb, TS + 2 * HALO, LANES), F32),
                        pltpu.VMEM((ncb, TS, LANES), F32)],
        compiler_params=_params(("parallel", "parallel")),
        name="conv_out",
    )(glu, glu, glu, sg, x1, cw, cb, ng, nb, wo, fg)


def _t5_bucket_np(rel):
    nb = T5_BUCKETS // 2
    max_exact = nb // 2
    ret = (rel > 0).astype(np.int64) * nb
    n = np.abs(rel)
    nf = np.maximum(n, 1).astype(np.float64)
    large = max_exact + (np.log(nf / max_exact) / math.log(T5_MAX_DIST / max_exact)
                         * (nb - max_exact)).astype(np.int64)
    large = np.minimum(large, nb - 1)
    return ret + np.where(n < max_exact, n, large)


def _rope_tables(S):
    half = RET_DK // 2
    inv = 1.0 / (ROPE_BASE ** (jnp.arange(0, RET_DK, 2, dtype=F32) / RET_DK))
    ang = jnp.arange(S).astype(F32)[:, None] * inv[None, :]
    cos = jnp.cos(ang)
    sin = jnp.sin(ang)
    assert cos.shape == (S, half)
    return jnp.concatenate([cos, cos], axis=-1), jnp.concatenate([-sin, sin], axis=-1)


def _trunk(x, p):
    B, S, D = x.shape
    assert D == D_MODEL and S % ATT_TILE == 0 and S % RET_CHUNK == 0 and S % ROW_TILE == 0

    cos, sin = _rope_tables(S)
    rq, rk, rv, rg, dq, dk, dvt, dg = _even_inproj(x, p["g0"], p["w_in_mix"], p["wvt"], cos, sin)
    ro = _retention(p["log_g"], rq, rk, rv, rg)
    do = _diff_attn(p["attn_scal"], dq, dk, dvt, p["btab"], dg, p["diff_norm_g"], p["lam_init"])
    x1, glu, sg = _mid_proj(x, ro, do, p["w_out_mix"], p["g1"], p["w_in_conv"])
    return _conv_out(glu, sg, x1, p["conv_w"], p["conv_b"], p["conv_norm_g"], p["conv_norm_b"],
                     p["w_out_conv"], p["final_g"])


def kernel(x_prompt, x_sample, norm_g, final_g, rel_bias, w_in_mix, ret_decay, lam_q1, lam_k1, lam_q2,
           lam_k2, diff_norm_g, w_out_mix, w_in_conv, conv_w, conv_b, conv_norm_g, conv_norm_b,
           w_out_conv):
    assert norm_g.shape[0] == 2 and w_in_mix.shape[0] == 1 and w_in_conv.shape[0] == 1
    lam_init = 0.8 - 0.6 * math.exp(-0.3 * 0)
    lam = (jnp.exp(jnp.sum(lam_q1[0].astype(F32) * lam_k1[0].astype(F32)))
           - jnp.exp(jnp.sum(lam_q2[0].astype(F32) * lam_k2[0].astype(F32))) + lam_init)
    T = ATT_TILE
    j = np.arange(2 * T)
    idx = np.stack([_t5_bucket_np(np.where(j <= T, d * T - j, d * T + 2 * T - j))
                    for d in (-1, 0, 1)]).astype(np.int32)
    bias2 = rel_bias.astype(F32) * LOG2E
    btab = jnp.take(bias2.T, jnp.asarray(idx), axis=1).reshape(N_DIFF_HEADS, 3, 1, 2 * T)
    far_left = int(_t5_bucket_np(np.array(-T5_MAX_DIST)))
    far_right = int(_t5_bucket_np(np.array(T5_MAX_DIST)))
    attn_scal = jnp.concatenate([lam.reshape(1), bias2[far_left], bias2[far_right]])
    wim = w_in_mix[0]
    ncb = CONV_W // LANES
    p = dict(
        g0=norm_g[0].reshape(1, D_MODEL), g1=norm_g[1].reshape(1, D_MODEL),
        final_g=final_g.reshape(1, D_MODEL),
        w_in_mix=wim.astype(BF16),
        wvt=wim[:, _C_DV:_C_DV + DIFF_W].T.astype(BF16),
        log_g=(-jnp.exp(ret_decay[0].astype(F32))).reshape(-1),
        attn_scal=attn_scal, btab=btab, lam_init=lam_init,
        diff_norm_g=diff_norm_g[0].reshape(1, DIFF_DV),
        w_out_mix=w_out_mix[0].astype(BF16),
        w_in_conv=w_in_conv[0].astype(BF16),
        conv_w=conv_w[0].reshape(CONV_K, ncb, LANES).transpose(1, 0, 2),
        conv_b=conv_b[0].reshape(1, CONV_W),
        conv_norm_g=conv_norm_g[0].reshape(1, CONV_W),
        conv_norm_b=conv_norm_b[0].reshape(1, CONV_W),
        w_out_conv=w_out_conv[0].astype(BF16),
    )
    return (_trunk(x_prompt, p), _trunk(x_sample, p))
```

```python
import functools
import math

import numpy as np
import jax
import jax.numpy as jnp
from jax import lax
from jax.experimental import pallas as pl
from jax.experimental.pallas import tpu as pltpu

F32 = jnp.float32
BF16 = jnp.bfloat16

D_MODEL = 1024
N_RET_HEADS = 4
RET_DK = 128
RET_DV = 256
N_DIFF_HEADS = 8
DIFF_DH = 64
DIFF_DV = 2 * DIFF_DH
RET_W = N_RET_HEADS * RET_DV
DIFF_W = N_DIFF_HEADS * DIFF_DV
CONV_W = 2 * D_MODEL
CONV_K = 31
CONV_PAD = CONV_K // 2
T5_BUCKETS = 32
T5_MAX_DIST = 128
ROPE_BASE = 10000.0
EPS = 1e-6

_C_RQ = 0
_C_RK = _C_RQ + N_RET_HEADS * RET_DK
_C_RV = _C_RK + N_RET_HEADS * RET_DK
_C_RG = _C_RV + RET_W
_C_DQ = _C_RG + RET_W
_C_DK = _C_DQ + DIFF_W
_C_DV = _C_DK + DIFF_W
_C_DG = _C_DV + DIFF_W
EVEN_IN_W = _C_DG + DIFF_W

LANES = 128
HALO = 16
VMEM_LIMIT = 56 * 1024 * 1024

ROW_TILE = 256
RET_CHUNK = 256
RET_BLOCK = 512
ATT_TILE = 512
ATT_QBLOCKS = 2
ATT_NEAR = ATT_QBLOCKS + 2
ATT_ROWS = 32
ONES_ROWS = 16
LOG2E = math.log2(math.e)
CONV_TILE = 256
CONV_ROWS = 64

M_INIT = -1e30


def _rms(x, g):
    return x * lax.rsqrt(jnp.mean(x * x, axis=-1, keepdims=True) + EPS) * g


def _silu(x):
    return x * jax.nn.sigmoid(x)


def _params(sem):
    return pltpu.CompilerParams(dimension_semantics=sem, vmem_limit_bytes=VMEM_LIMIT)


def _whole(shape):
    nd = len(shape)
    return pl.BlockSpec(shape, lambda *_: (0,) * nd, pipeline_mode=pl.Buffered(1))


def _even_inproj_kernel(x_ref, g_ref, w_ref, wvt_ref, cos_ref, sin_ref,
                        rq_ref, rk_ref, rv_ref, rg_ref, dq_ref, dk_ref, dvt_ref, dg_ref):
    hb = _rms(x_ref[...], g_ref[...]).astype(BF16)

    def proj(c0, n):
        return jnp.dot(hb, w_ref[:, c0:c0 + n], preferred_element_type=F32)

    cos = cos_ref[...]
    sin = sin_ref[...]

    def rope(u):
        return u * cos + pltpu.roll(u, RET_DK // 2, 1) * sin

    uq = proj(_C_RQ, N_RET_HEADS * RET_DK)
    uk = proj(_C_RK, N_RET_HEADS * RET_DK)
    for j in range(N_RET_HEADS):
        sl = slice(j * RET_DK, (j + 1) * RET_DK)
        rq_ref[:, sl] = rope(uq[:, sl]).astype(BF16)
        rk_ref[:, sl] = (rope(uk[:, sl]) * (RET_DK ** -0.5)).astype(BF16)
    rv_ref[...] = proj(_C_RV, RET_W).astype(BF16)
    rg_ref[...] = proj(_C_RG, RET_W)
    dq_ref[...] = (proj(_C_DQ, DIFF_W) * (DIFF_DH ** -0.5 * LOG2E)).astype(BF16)
    dk_ref[...] = proj(_C_DK, DIFF_W).astype(BF16)
    dg_ref[...] = proj(_C_DG, DIFF_W)
    dvt_ref[...] = lax.dot_general(wvt_ref[...], hb, (((1,), (1,)), ((), ())),
                                   preferred_element_type=F32).astype(BF16)


def _even_inproj(x, g, w, wvt, cos, sin):
    B, S, D = x.shape
    tm = ROW_TILE
    row = lambda n: pl.BlockSpec((None, tm, n), lambda b, i: (b, i, 0))
    tab = pl.BlockSpec((tm, LANES), lambda b, i: (i, 0))
    out_shape = (
        jax.ShapeDtypeStruct((B, S, N_RET_HEADS * RET_DK), BF16),
        jax.ShapeDtypeStruct((B, S, N_RET_HEADS * RET_DK), BF16),
        jax.ShapeDtypeStruct((B, S, RET_W), BF16),
        jax.ShapeDtypeStruct((B, S, RET_W), F32),
        jax.ShapeDtypeStruct((B, S, DIFF_W), BF16),
        jax.ShapeDtypeStruct((B, S, DIFF_W), BF16),
        jax.ShapeDtypeStruct((B, S // ATT_TILE, DIFF_W, ATT_TILE), BF16),
        jax.ShapeDtypeStruct((B, S, DIFF_W), F32),
    )
    per_tile = ATT_TILE // tm
    out_specs = (row(N_RET_HEADS * RET_DK), row(N_RET_HEADS * RET_DK), row(RET_W), row(RET_W),
                 row(DIFF_W), row(DIFF_W),
                 pl.BlockSpec((None, None, DIFF_W, tm), lambda b, i: (b, i // per_tile, 0, i % per_tile)),
                 row(DIFF_W))
    return pl.pallas_call(
        _even_inproj_kernel,
        grid=(B, S // tm),
        in_specs=[row(D), _whole((1, D)), _whole(w.shape), _whole(wvt.shape), tab, tab],
        out_specs=out_specs,
        out_shape=out_shape,
        compiler_params=_params(("parallel", "parallel")),
        name="even_inproj",
    )(x, g, w, wvt, cos, sin)


def _retention_kernel(lg_ref, q_ref, k_ref, v_ref, g_ref, o_ref, lst_ref, st_ref, dmat_ref):
    n = pl.program_id(1)
    nblk = pl.num_programs(1) // 2
    C = RET_CHUNK
    cpb = RET_BLOCK // C
    heads = range(N_RET_HEADS)
    lf = [lg_ref[h] for h in heads]
    lb = [lg_ref[N_RET_HEADS + h] for h in heads]
    pos = lax.broadcasted_iota(jnp.int32, (C, 1), 0).astype(F32)

    def kcols(h):
        return slice(h * RET_DK, (h + 1) * RET_DK)

    def vcols(h):
        return slice(h * RET_DV, (h + 1) * RET_DV)

    def decay(x):
        return jnp.exp(jnp.full((1, RET_DV), x, F32))

    def kt_v(h, rows, w_col):
        kw = (k_ref[rows, kcols(h)].astype(F32) * w_col).T.astype(BF16)
        return jnp.dot(kw, v_ref[rows, vcols(h)], preferred_element_type=F32)

    @pl.when(n == 0)
    def _():
        r = lax.broadcasted_iota(jnp.int32, (C, C), 0)
        c = lax.broadcasted_iota(jnp.int32, (C, C), 1)
        d = (r - c).astype(F32)
        for h in heads:
            dmat_ref[h] = jnp.where(d >= 0, jnp.exp(lf[h] * jnp.maximum(d, 0.0)),
                                    jnp.exp(lb[h] * jnp.maximum(-d, 0.0)))

    @pl.when((n == 0) | (n == nblk))
    def _():
        st_ref[...] = jnp.zeros_like(st_ref)

    @pl.when(n < nblk)
    def _():
        blk = nblk - 1 - n
        for cc in reversed(range(cpb)):
            rows = slice(cc * C, (cc + 1) * C)
            chunk = blk * cpb + cc
            upd = [kt_v(h, rows, jnp.exp(lb[h] * pos)) for h in heads]
            for h in heads:
                lst_ref[chunk, h] = st_ref[h].astype(BF16)
                st_ref[h] = decay(lb[h] * C) * st_ref[h] + upd[h]

    @pl.when(n >= nblk)
    def _():
        blk = n - nblk
        for cc in range(cpb):
            rows = slice(cc * C, (cc + 1) * C)
            chunk = blk * cpb + cc
            q = [q_ref[rows, kcols(h)] for h in heads]
            s = [lax.dot_general(q[h], k_ref[rows, kcols(h)], (((1,), (1,)), ((), ())),
                                 preferred_element_type=F32) for h in heads]
            qf = [(q[h].astype(F32) * jnp.exp(lf[h] * (pos + 1.0))).astype(BF16) for h in heads]
            qb = [(q[h].astype(F32) * jnp.exp(lb[h] * (C - pos))).astype(BF16) for h in heads]
            sd = [(s[h] * dmat_ref[h]).astype(BF16) for h in heads]
            o = [jnp.dot(sd[h], v_ref[rows, vcols(h)], preferred_element_type=F32)
                 + jnp.dot(qf[h], st_ref[h].astype(BF16), preferred_element_type=F32)
                 + jnp.dot(qb[h], lst_ref[chunk, h], preferred_element_type=F32) for h in heads]
            upd = [kt_v(h, rows, jnp.exp(lf[h] * (C - 1.0 - pos))) for h in heads]
            for h in heads:
                st_ref[h] = decay(lf[h] * C) * st_ref[h] + upd[h]
                oh = o[h] * lax.rsqrt(jnp.mean(o[h] * o[h], axis=-1, keepdims=True) + EPS)
                o_ref[rows, vcols(h)] = (oh * _silu(g_ref[rows, vcols(h)])).astype(BF16)


def _retention(log_g, rq, rk, rv, rg):
    B, S, _ = rq.shape
    RB = RET_BLOCK
    NB = S // RB
    KW = N_RET_HEADS * RET_DK

    def both(n):
        return jnp.where(n < NB, NB - 1 - n, n - NB)

    def fwd(n):
        return jnp.maximum(n - NB, 0)

    q_spec = pl.BlockSpec((None, RB, KW), lambda b, n, lg: (b, fwd(n), 0))
    k_spec = pl.BlockSpec((None, RB, KW), lambda b, n, lg: (b, both(n), 0))
    v_spec = pl.BlockSpec((None, RB, RET_W), lambda b, n, lg: (b, both(n), 0))
    fwd_spec = pl.BlockSpec((None, RB, RET_W), lambda b, n, lg: (b, fwd(n), 0))
    grid_spec = pltpu.PrefetchScalarGridSpec(
        num_scalar_prefetch=1,
        grid=(B, 2 * NB),
        in_specs=[q_spec, k_spec, v_spec, fwd_spec],
        out_specs=fwd_spec,
        scratch_shapes=[pltpu.VMEM((S // RET_CHUNK, N_RET_HEADS, RET_DK, RET_DV), BF16),
                        pltpu.VMEM((N_RET_HEADS, RET_DK, RET_DV), F32),
                        pltpu.VMEM((N_RET_HEADS, RET_CHUNK, RET_CHUNK), F32)],
    )
    return pl.pallas_call(
        _retention_kernel,
        grid_spec=grid_spec,
        out_shape=jax.ShapeDtypeStruct((B, S, RET_W), BF16),
        compiler_params=_params(("parallel", "arbitrary")),
        name="retention",
    )(log_g, rq, rk, rv, rg)


def _diff_attn_kernel(sc_ref, q_ref, k_ref, vt_ref, btab_ref, g_ref, ng_ref, o_ref,
                      qz_ref, bias_ref, m_ref, acc_ref, s0_ref, p_ref, *, lam_init):
    h = pl.program_id(1)
    qi = pl.program_id(2)
    T = ATT_TILE
    TQ = ATT_QBLOCKS * T
    nk = vt_ref.shape[0]
    lam = sc_ref[0]
    bias_left = sc_ref[1 + h]
    bias_right = sc_ref[1 + N_DIFF_HEADS + h]

    @pl.when(qi == 0)
    def _():
        for d in range(ATT_NEAR):
            x = jnp.broadcast_to(btab_ref[d], (T, btab_ref.shape[-1]))
            bias_ref[d] = pltpu.roll(x, 0, 1, stride=1, stride_axis=0)[:, :TQ]

    q = q_ref[...]
    lane = lax.broadcasted_iota(jnp.int32, q.shape, 1)
    zero = jnp.zeros_like(q)
    qz_ref[0] = jnp.where(lane < DIFF_DH, q, zero)
    qz_ref[1] = jnp.where(lane >= DIFF_DH, q, zero)
    m_ref[...] = jnp.full_like(m_ref, M_INIT)
    acc_ref[...] = jnp.zeros_like(acc_ref)
    ones = jnp.ones((ONES_ROWS, T), BF16)

    chains = [(c, slice(qb * T, (qb + 1) * T)) for qb in range(ATT_QBLOCKS) for c in range(2)]

    def scores(chain, j):
        c, cols = chain
        k = k_ref[pl.ds(pl.multiple_of(j * T, T), T), :]
        return lax.dot_general(k, qz_ref[c, cols, :], (((1,), (1,)), ((), ())),
                               preferred_element_type=F32)

    def softmax_pv(chain, vt, s, d, p_ref):
        c, cols = chain

        def rows(r):
            sr = s[r:r + ATT_ROWS]
            return sr if d is None else sr + bias_ref[d, r:r + ATT_ROWS, cols]

        mx = rows(0)
        for r in range(ATT_ROWS, T, ATT_ROWS):
            mx = jnp.maximum(mx, rows(r))
        m_old = m_ref[c, :, cols]
        m_new = jnp.maximum(m_old, jnp.max(mx, axis=0, keepdims=True))
        for r in range(0, T, ATT_ROWS):
            p_ref[r:r + ATT_ROWS, :] = jnp.exp2(rows(r) - m_new).astype(BF16)
        acc_ref[c, :, cols] = (jnp.exp2(m_old - m_new) * acc_ref[c, :, cols]
                               + jnp.dot(vt, p_ref[...], preferred_element_type=F32))
        m_ref[c, :, cols] = m_new

    def tile(j, d):
        vt = jnp.concatenate([vt_ref[j], ones], axis=0)
        s_cur = s0_ref[...]
        for i, chain in enumerate(chains):
            if i + 1 < len(chains):
                s_next = scores(chains[i + 1], j)
            else:
                s0_ref[...] = scores(chains[0], jnp.minimum(j + 1, nk - 1))
            softmax_pv(chain, vt, s_cur, d, p_ref.at[i % 2])
            s_cur = s_next

    def far_tile(j, carry):
        tile(j, None)
        return carry

    first_near = ATT_QBLOCKS * qi - 1
    s0_ref[...] = scores(chains[0], 0)
    lax.fori_loop(0, first_near, far_tile, 0)
    m_ref[...] = m_ref[...] + bias_left
    for d in range(ATT_NEAR):
        j = first_near + d

        @pl.when((j >= 0) & (j < nk))
        def _():
            tile(jnp.clip(j, 0, nk - 1), d)

    m_ref[...] = m_ref[...] - bias_right
    lax.fori_loop(first_near + ATT_NEAR, nk, far_tile, 0)

    def normalised(c):
        return acc_ref[c, :DIFF_DV, :] / acc_ref[c, DIFF_DV:DIFF_DV + 1, :]

    ot = normalised(0) - lam * normalised(1)
    o = _rms(ot.T, ng_ref[...]) * (1.0 - lam_init)
    o_ref[...] = (o * _silu(g_ref[...])).astype(BF16)


def _diff_attn(scal, dq, dk, dvt, btab, dg, ng, lam_init):
    B, S, _ = dq.shape
    T = ATT_TILE
    TQ = ATT_QBLOCKS * T
    q_spec = pl.BlockSpec((None, TQ, DIFF_DV), lambda b, h, qi, sc: (b, qi, h))
    k_spec = pl.BlockSpec((None, S, DIFF_DV), lambda b, h, qi, sc: (b, 0, h))
    vt_spec = pl.BlockSpec((None, S // T, DIFF_DV, T), lambda b, h, qi, sc: (b, 0, h, 0))
    btab_spec = pl.BlockSpec((None, ATT_NEAR, 1, btab.shape[-1]), lambda b, h, qi, sc: (h, 0, 0, 0))
    ng_spec = pl.BlockSpec((1, DIFF_DV), lambda b, h, qi, sc: (0, 0))
    grid_spec = pltpu.PrefetchScalarGridSpec(
        num_scalar_prefetch=1,
        grid=(B, N_DIFF_HEADS, S // TQ),
        in_specs=[q_spec, k_spec, vt_spec, btab_spec, q_spec, ng_spec],
        out_specs=q_spec,
        scratch_shapes=[pltpu.VMEM((2, TQ, DIFF_DV), BF16),
                        pltpu.VMEM((ATT_NEAR, T, TQ), F32),
                        pltpu.VMEM((2, 1, TQ), F32),
                        pltpu.VMEM((2, DIFF_DV + ONES_ROWS, TQ), F32),
                        pltpu.VMEM((T, T), F32),
                        pltpu.VMEM((2, T, T), BF16)],
    )
    return pl.pallas_call(
        functools.partial(_diff_attn_kernel, lam_init=lam_init),
        grid_spec=grid_spec,
        out_shape=jax.ShapeDtypeStruct((B, S, DIFF_W), BF16),
        compiler_params=_params(("parallel", "parallel", "arbitrary")),
        name="diff_attn",
    )(scal, dq, dk, dvt, btab, dg, ng)


def _mid_proj_kernel(x_ref, ro_ref, do_ref, wo_ref, g_ref, wi_ref, x1_ref, glu_ref, sg_ref):
    x1 = (x_ref[...]
          + jnp.dot(ro_ref[...], wo_ref[:RET_W, :], preferred_element_type=F32)
          + jnp.dot(do_ref[...], wo_ref[RET_W:, :], preferred_element_type=F32))
    x1_ref[...] = x1
    hb = _rms(x1, g_ref[...]).astype(BF16)
    step = 512
    for c0 in range(0, CONV_W, step):
        a = jnp.dot(hb, wi_ref[:, c0:c0 + step], preferred_element_type=F32)
        b = jnp.dot(hb, wi_ref[:, CONV_W + c0:CONV_W + c0 + step], preferred_element_type=F32)
        gate = jnp.dot(hb, wi_ref[:, 2 * CONV_W + c0:2 * CONV_W + c0 + step], preferred_element_type=F32)
        glu_ref[:, c0:c0 + step] = a * jax.nn.sigmoid(b)
        sg_ref[:, c0:c0 + step] = _silu(gate)


def _mid_proj(x, ro, do, wo, g, wi):
    B, S, D = x.shape
    tm = ROW_TILE
    row = lambda n: pl.BlockSpec((None, tm, n), lambda b, i: (b, i, 0))
    return pl.pallas_call(
        _mid_proj_kernel,
        grid=(B, S // tm),
        in_specs=[row(D), row(RET_W), row(DIFF_W), _whole(wo.shape), _whole((1, D)), _whole(wi.shape)],
        out_specs=(row(D), row(CONV_W), row(CONV_W)),
        out_shape=(jax.ShapeDtypeStruct((B, S, D), F32),
                   jax.ShapeDtypeStruct((B, S, CONV_W), F32),
                   jax.ShapeDtypeStruct((B, S, CONV_W), F32)),
        compiler_params=_params(("parallel", "parallel")),
        name="mid_proj",
    )(x, ro, do, wo, g, wi)


def _conv_out_kernel(glu_ref, prev_ref, next_ref, sg_ref, x1_ref, cw_ref, cb_ref, ng_ref, nb_ref,
                     wo_ref, fg_ref, o_ref, win_ref, y_ref):
    i = pl.program_id(1)
    nblk = pl.num_programs(1)
    TS = CONV_TILE
    ncb = CONV_W // LANES
    prev = jnp.where(i > 0, prev_ref[...], 0.0)
    nxt = jnp.where(i < nblk - 1, next_ref[...], 0.0)
    for cb in range(ncb):
        sl = slice(cb * LANES, (cb + 1) * LANES)
        win_ref[cb, 0:HALO, :] = prev[:, sl]
        win_ref[cb, HALO:HALO + TS, :] = glu_ref[:, sl]
        win_ref[cb, HALO + TS:HALO + TS + HALO, :] = nxt[:, sl]

    def chan_block(cb, carry):
        w = cw_ref[cb]
        for r0 in range(0, TS, CONV_ROWS):
            acc = jnp.zeros((CONV_ROWS, LANES), F32)
            for t in range(CONV_K):
                off = r0 + HALO - CONV_PAD + t
                acc = acc + win_ref[cb, pl.ds(off, CONV_ROWS), :] * w[t:t + 1, :]
            y_ref[cb, r0:r0 + CONV_ROWS, :] = acc
        return carry

    lax.fori_loop(0, ncb, chan_block, 0)

    y = jnp.concatenate([y_ref[cb] for cb in range(ncb)], axis=-1) + cb_ref[...]
    mu = jnp.mean(y, axis=-1, keepdims=True)
    yc = y - mu
    var = jnp.mean(yc * yc, axis=-1, keepdims=True)
    yn = yc * lax.rsqrt(var + EPS) * ng_ref[...] + nb_ref[...]
    z = (_silu(yn) * sg_ref[...]).astype(BF16)
    out = x1_ref[...] + jnp.dot(z, wo_ref[...], preferred_element_type=F32)
    o_ref[...] = _rms(out, fg_ref[...])


def _conv_out(glu, sg, x1, cw, cb, ng, nb, wo, fg):
    B, S, D = x1.shape
    TS = CONV_TILE
    hb = TS // HALO
    nh = S // HALO
    row = lambda n: pl.BlockSpec((None, TS, n), lambda b, i: (b, i, 0))
    prev_spec = pl.BlockSpec((None, HALO, CONV_W), lambda b, i: (b, jnp.maximum(i * hb - 1, 0), 0))
    next_spec = pl.BlockSpec((None, HALO, CONV_W), lambda b, i: (b, jnp.minimum((i + 1) * hb, nh - 1), 0))
    ncb = CONV_W // LANES
    return pl.pallas_call(
        _conv_out_kernel,
        grid=(B, S // TS),
        in_specs=[row(CONV_W), prev_spec, next_spec, row(CONV_W), row(D),
                  _whole(cw.shape), _whole((1, CONV_W)), _whole((1, CONV_W)), _whole((1, CONV_W)),
                  _whole(wo.shape), _whole((1, D))],
        out_specs=row(D),
        out_shape=jax.ShapeDtypeStruct((B, S, D), F32),
        scratch_shapes=[pltpu.VMEM((ncb, TS + 2 * HALO, LANES), F32),
                        pltpu.VMEM((ncb, TS, LANES), F32)],
        compiler_params=_params(("parallel", "parallel")),
        name="conv_out",
    )(glu, glu, glu, sg, x1, cw, cb, ng, nb, wo, fg)


def _t5_bucket_np(rel):
    nb = T5_BUCKETS // 2
    max_exact = nb // 2
    ret = (rel > 0).astype(np.int64) * nb
    n = np.abs(rel)
    nf = np.maximum(n, 1).astype(np.float64)
    large = max_exact + (np.log(nf / max_exact) / math.log(T5_MAX_DIST / max_exact)
                         * (nb - max_exact)).astype(np.int64)
    large = np.minimum(large, nb - 1)
    return ret + np.where(n < max_exact, n, large)


def _rope_tables(S):
    half = RET_DK // 2
    inv = 1.0 / (ROPE_BASE ** (jnp.arange(0, RET_DK, 2, dtype=F32) / RET_DK))
    ang = jnp.arange(S).astype(F32)[:, None] * inv[None, :]
    cos = jnp.cos(ang)
    sin = jnp.sin(ang)
    assert cos.shape == (S, half)
    return jnp.concatenate([cos, cos], axis=-1), jnp.concatenate([-sin, sin], axis=-1)


def _trunk(x, p):
    B, S, D = x.shape
    assert D == D_MODEL and S % ATT_TILE == 0 and S % RET_CHUNK == 0 and S % ROW_TILE == 0

    cos, sin = _rope_tables(S)
    rq, rk, rv, rg, dq, dk, dvt, dg = _even_inproj(x, p["g0"], p["w_in_mix"], p["wvt"], cos, sin)
    ro = _retention(p["log_g"], rq, rk, rv, rg)
    do = _diff_attn(p["attn_scal"], dq, dk, dvt, p["btab"], dg, p["diff_norm_g"], p["lam_init"])
    x1, glu, sg = _mid_proj(x, ro, do, p["w_out_mix"], p["g1"], p["w_in_conv"])
    return _conv_out(glu, sg, x1, p["conv_w"], p["conv_b"], p["conv_norm_g"], p["conv_norm_b"],
                     p["w_out_conv"], p["final_g"])


def kernel(x_prompt, x_sample, norm_g, final_g, rel_bias, w_in_mix, ret_decay, lam_q1, lam_k1, lam_q2,
           lam_k2, diff_norm_g, w_out_mix, w_in_conv, conv_w, conv_b, conv_norm_g, conv_norm_b,
           w_out_conv):
    assert norm_g.shape[0] == 2 and w_in_mix.shape[0] == 1 and w_in_conv.shape[0] == 1
    lam_init = 0.8 - 0.6 * math.exp(-0.3 * 0)
    lam = (jnp.exp(jnp.sum(lam_q1[0].astype(F32) * lam_k1[0].astype(F32)))
           - jnp.exp(jnp.sum(lam_q2[0].astype(F32) * lam_k2[0].astype(F32))) + lam_init)
    T = ATT_TILE
    TQ = ATT_QBLOCKS * T
    L = TQ + T
    i = np.arange(L)
    idx = np.stack([_t5_bucket_np(np.where(i <= TQ, off * T - i, off * T + L - i))
                    for off in range(-1, ATT_QBLOCKS + 1)]).astype(np.int32)
    bias2 = rel_bias.astype(F32) * LOG2E
    btab = jnp.take(bias2.T, jnp.asarray(idx), axis=1).reshape(N_DIFF_HEADS, ATT_NEAR, 1, L)
    far_left = int(_t5_bucket_np(np.array(-T5_MAX_DIST)))
    far_right = int(_t5_bucket_np(np.array(T5_MAX_DIST)))
    attn_scal = jnp.concatenate([lam.reshape(1), bias2[far_left], bias2[far_right]])
    wim = w_in_mix[0]
    ncb = CONV_W // LANES
    p = dict(
        g0=norm_g[0].reshape(1, D_MODEL), g1=norm_g[1].reshape(1, D_MODEL),
        final_g=final_g.reshape(1, D_MODEL),
        w_in_mix=wim.astype(BF16),
        wvt=wim[:, _C_DV:_C_DV + DIFF_W].T.astype(BF16),
        log_g=(-jnp.exp(ret_decay[0].astype(F32))).reshape(-1),
        attn_scal=attn_scal, btab=btab, lam_init=lam_init,
        diff_norm_g=diff_norm_g[0].reshape(1, DIFF_DV),
        w_out_mix=w_out_mix[0].astype(BF16),
        w_in_conv=w_in_conv[0].astype(BF16),
        conv_w=conv_w[0].reshape(CONV_K, ncb, LANES).transpose(1, 0, 2),
        conv_b=conv_b[0].reshape(1, CONV_W),
        conv_norm_g=conv_norm_g[0].reshape(1, CONV_W),
        conv_norm_b=conv_norm_b[0].reshape(1, CONV_W),
        w_out_conv=w_out_conv[0].astype(BF16),
    )
    return (_trunk(x_prompt, p), _trunk(x_sample, p))
```

```python
import functools
import math

import numpy as np
import jax
import jax.numpy as jnp
from jax import lax
from jax.experimental import pallas as pl
from jax.experimental.pallas import tpu as pltpu

F32 = jnp.float32
BF16 = jnp.bfloat16

D_MODEL = 1024
N_RET_HEADS = 4
RET_DK = 128
RET_DV = 256
N_DIFF_HEADS = 8
DIFF_DH = 64
DIFF_DV = 2 * DIFF_DH
RET_W = N_RET_HEADS * RET_DV
DIFF_W = N_DIFF_HEADS * DIFF_DV
CONV_W = 2 * D_MODEL
CONV_K = 31
CONV_PAD = CONV_K // 2
T5_BUCKETS = 32
T5_MAX_DIST = 128
ROPE_BASE = 10000.0
EPS = 1e-6

_C_RQ = 0
_C_RK = _C_RQ + N_RET_HEADS * RET_DK
_C_RV = _C_RK + N_RET_HEADS * RET_DK
_C_RG = _C_RV + RET_W
_C_DQ = _C_RG + RET_W
_C_DK = _C_DQ + DIFF_W
_C_DV = _C_DK + DIFF_W
_C_DG = _C_DV + DIFF_W
EVEN_IN_W = _C_DG + DIFF_W

LANES = 128
HALO = 16
VMEM_LIMIT = 56 * 1024 * 1024

ROW_TILE = 256
RET_CHUNK = 256
RET_BLOCK = 512
ATT_TILE = 512
ATT_QBLOCKS = 2
ATT_ROWS = 32
ATT_DEPTH = 2
ONES_ROWS = 16
LOG2E = math.log2(math.e)
CONV_TILE = 256
CONV_ROWS = 64

M_INIT = -1e30


def _rms(x, g):
    return x * lax.rsqrt(jnp.mean(x * x, axis=-1, keepdims=True) + EPS) * g


def _silu(x):
    return x * jax.nn.sigmoid(x)


def _params(sem):
    return pltpu.CompilerParams(dimension_semantics=sem, vmem_limit_bytes=VMEM_LIMIT)


def _whole(shape):
    nd = len(shape)
    return pl.BlockSpec(shape, lambda *_: (0,) * nd, pipeline_mode=pl.Buffered(1))


def _even_inproj_kernel(x_ref, g_ref, w_ref, wvt_ref, cos_ref, sin_ref,
                        rq_ref, rk_ref, rv_ref, rg_ref, dq_ref, dk_ref, dvt_ref, dg_ref):
    hb = _rms(x_ref[...], g_ref[...]).astype(BF16)

    def proj(c0, n):
        return jnp.dot(hb, w_ref[:, c0:c0 + n], preferred_element_type=F32)

    cos = cos_ref[...]
    sin = sin_ref[...]

    def rope(u):
        return u * cos + pltpu.roll(u, RET_DK // 2, 1) * sin

    uq = proj(_C_RQ, N_RET_HEADS * RET_DK)
    uk = proj(_C_RK, N_RET_HEADS * RET_DK)
    for j in range(N_RET_HEADS):
        sl = slice(j * RET_DK, (j + 1) * RET_DK)
        rq_ref[:, sl] = rope(uq[:, sl]).astype(BF16)
        rk_ref[:, sl] = (rope(uk[:, sl]) * (RET_DK ** -0.5)).astype(BF16)
    rv_ref[...] = proj(_C_RV, RET_W).astype(BF16)
    rg_ref[...] = proj(_C_RG, RET_W)
    dq_ref[...] = (proj(_C_DQ, DIFF_W) * (DIFF_DH ** -0.5 * LOG2E)).astype(BF16)
    dk_ref[...] = proj(_C_DK, DIFF_W).astype(BF16)
    dg_ref[...] = proj(_C_DG, DIFF_W)
    dvt_ref[...] = lax.dot_general(wvt_ref[...], hb, (((1,), (1,)), ((), ())),
                                   preferred_element_type=F32).astype(BF16)


def _even_inproj(x, g, w, wvt, cos, sin):
    B, S, D = x.shape
    tm = ROW_TILE
    row = lambda n: pl.BlockSpec((None, tm, n), lambda b, i: (b, i, 0))
    tab = pl.BlockSpec((tm, LANES), lambda b, i: (i, 0))
    out_shape = (
        jax.ShapeDtypeStruct((B, S, N_RET_HEADS * RET_DK), BF16),
        jax.ShapeDtypeStruct((B, S, N_RET_HEADS * RET_DK), BF16),
        jax.ShapeDtypeStruct((B, S, RET_W), BF16),
        jax.ShapeDtypeStruct((B, S, RET_W), F32),
        jax.ShapeDtypeStruct((B, S, DIFF_W), BF16),
        jax.ShapeDtypeStruct((B, S, DIFF_W), BF16),
        jax.ShapeDtypeStruct((B, S // ATT_TILE, DIFF_W, ATT_TILE), BF16),
        jax.ShapeDtypeStruct((B, S, DIFF_W), F32),
    )
    per_tile = ATT_TILE // tm
    out_specs = (row(N_RET_HEADS * RET_DK), row(N_RET_HEADS * RET_DK), row(RET_W), row(RET_W),
                 row(DIFF_W), row(DIFF_W),
                 pl.BlockSpec((None, None, DIFF_W, tm), lambda b, i: (b, i // per_tile, 0, i % per_tile)),
                 row(DIFF_W))
    return pl.pallas_call(
        _even_inproj_kernel,
        grid=(B, S // tm),
        in_specs=[row(D), _whole((1, D)), _whole(w.shape), _whole(wvt.shape), tab, tab],
        out_specs=out_specs,
        out_shape=out_shape,
        compiler_params=_params(("parallel", "parallel")),
        name="even_inproj",
    )(x, g, w, wvt, cos, sin)


def _retention_kernel(lg_ref, q_ref, k_ref, v_ref, g_ref, o_ref, lst_ref, st_ref, dmat_ref):
    n = pl.program_id(1)
    nblk = pl.num_programs(1) // 2
    C = RET_CHUNK
    cpb = RET_BLOCK // C
    heads = range(N_RET_HEADS)
    lf = [lg_ref[h] for h in heads]
    lb = [lg_ref[N_RET_HEADS + h] for h in heads]
    pos = lax.broadcasted_iota(jnp.int32, (C, 1), 0).astype(F32)

    def kcols(h):
        return slice(h * RET_DK, (h + 1) * RET_DK)

    def vcols(h):
        return slice(h * RET_DV, (h + 1) * RET_DV)

    def decay(x):
        return jnp.exp(jnp.full((1, RET_DV), x, F32))

    def kt_v(h, rows, w_col):
        kw = (k_ref[rows, kcols(h)].astype(F32) * w_col).T.astype(BF16)
        return jnp.dot(kw, v_ref[rows, vcols(h)], preferred_element_type=F32)

    @pl.when(n == 0)
    def _():
        r = lax.broadcasted_iota(jnp.int32, (C, C), 0)
        c = lax.broadcasted_iota(jnp.int32, (C, C), 1)
        d = (r - c).astype(F32)
        for h in heads:
            dmat_ref[h] = jnp.where(d >= 0, jnp.exp(lf[h] * jnp.maximum(d, 0.0)),
                                    jnp.exp(lb[h] * jnp.maximum(-d, 0.0)))

    @pl.when((n == 0) | (n == nblk))
    def _():
        st_ref[...] = jnp.zeros_like(st_ref)

    @pl.when(n < nblk)
    def _():
        blk = nblk - 1 - n
        for cc in reversed(range(cpb)):
            rows = slice(cc * C, (cc + 1) * C)
            chunk = blk * cpb + cc
            upd = [kt_v(h, rows, jnp.exp(lb[h] * pos)) for h in heads]
            for h in heads:
                lst_ref[chunk, h] = st_ref[h].astype(BF16)
                st_ref[h] = decay(lb[h] * C) * st_ref[h] + upd[h]

    @pl.when(n >= nblk)
    def _():
        blk = n - nblk
        for cc in range(cpb):
            rows = slice(cc * C, (cc + 1) * C)
            chunk = blk * cpb + cc
            q = [q_ref[rows, kcols(h)] for h in heads]
            s = [lax.dot_general(q[h], k_ref[rows, kcols(h)], (((1,), (1,)), ((), ())),
                                 preferred_element_type=F32) for h in heads]
            qf = [(q[h].astype(F32) * jnp.exp(lf[h] * (pos + 1.0))).astype(BF16) for h in heads]
            qb = [(q[h].astype(F32) * jnp.exp(lb[h] * (C - pos))).astype(BF16) for h in heads]
            sd = [(s[h] * dmat_ref[h]).astype(BF16) for h in heads]
            o = [jnp.dot(sd[h], v_ref[rows, vcols(h)], preferred_element_type=F32)
                 + jnp.dot(qf[h], st_ref[h].astype(BF16), preferred_element_type=F32)
                 + jnp.dot(qb[h], lst_ref[chunk, h], preferred_element_type=F32) for h in heads]
            upd = [kt_v(h, rows, jnp.exp(lf[h] * (C - 1.0 - pos))) for h in heads]
            for h in heads:
                st_ref[h] = decay(lf[h] * C) * st_ref[h] + upd[h]
                oh = o[h] * lax.rsqrt(jnp.mean(o[h] * o[h], axis=-1, keepdims=True) + EPS)
                o_ref[rows, vcols(h)] = (oh * _silu(g_ref[rows, vcols(h)])).astype(BF16)


def _retention(log_g, rq, rk, rv, rg):
    B, S, _ = rq.shape
    RB = RET_BLOCK
    NB = S // RB
    KW = N_RET_HEADS * RET_DK

    def both(n):
        return jnp.where(n < NB, NB - 1 - n, n - NB)

    def fwd(n):
        return jnp.maximum(n - NB, 0)

    q_spec = pl.BlockSpec((None, RB, KW), lambda b, n, lg: (b, fwd(n), 0))
    k_spec = pl.BlockSpec((None, RB, KW), lambda b, n, lg: (b, both(n), 0))
    v_spec = pl.BlockSpec((None, RB, RET_W), lambda b, n, lg: (b, both(n), 0))
    fwd_spec = pl.BlockSpec((None, RB, RET_W), lambda b, n, lg: (b, fwd(n), 0))
    grid_spec = pltpu.PrefetchScalarGridSpec(
        num_scalar_prefetch=1,
        grid=(B, 2 * NB),
        in_specs=[q_spec, k_spec, v_spec, fwd_spec],
        out_specs=fwd_spec,
        scratch_shapes=[pltpu.VMEM((S // RET_CHUNK, N_RET_HEADS, RET_DK, RET_DV), BF16),
                        pltpu.VMEM((N_RET_HEADS, RET_DK, RET_DV), F32),
                        pltpu.VMEM((N_RET_HEADS, RET_CHUNK, RET_CHUNK), F32)],
    )
    return pl.pallas_call(
        _retention_kernel,
        grid_spec=grid_spec,
        out_shape=jax.ShapeDtypeStruct((B, S, RET_W), BF16),
        compiler_params=_params(("parallel", "arbitrary")),
        name="retention",
    )(log_g, rq, rk, rv, rg)


def _diff_attn_kernel(sc_ref, q_ref, k_ref, vt_ref, btab_ref, g_ref, ng_ref, o_ref,
                      qz_ref, bias_ref, m_ref, acc_ref, s0_ref, p_ref, *, lam_init):
    h = pl.program_id(1)
    qi = pl.program_id(2)
    T = ATT_TILE
    nk = vt_ref.shape[0]
    lam = sc_ref[0]
    bias_left = sc_ref[1 + h]
    bias_right = sc_ref[1 + N_DIFF_HEADS + h]

    @pl.when(qi == 0)
    def _():
        for d in range(3):
            x = jnp.broadcast_to(btab_ref[d], (T, 2 * T))
            bias_ref[d] = pltpu.roll(x, 0, 1, stride=1, stride_axis=0)[:, :T]

    qt = q_ref[...].astype(F32).T
    half = lax.broadcasted_iota(jnp.int32, qt.shape, 0) < DIFF_DH
    qz_ref[0] = jnp.where(half, qt, 0.0).astype(BF16)
    qz_ref[1] = jnp.where(half, 0.0, qt).astype(BF16)
    m_ref[...] = jnp.full_like(m_ref, M_INIT)
    acc_ref[...] = jnp.zeros_like(acc_ref)
    ones = jnp.ones((ONES_ROWS, T), BF16)

    chains = [(ko, qb, c) for ko in range(ATT_QBLOCKS) for qb in range(ATT_QBLOCKS) for c in range(2)]

    def qcols(qb):
        return slice(qb * T, (qb + 1) * T)

    def scores(chain, kp):
        ko, qb, c = chain
        row0 = pl.multiple_of((kp * ATT_QBLOCKS + ko) * T, T)
        k = k_ref[pl.ds(row0, T), :]
        return jnp.dot(k, qz_ref[c, :, qcols(qb)], preferred_element_type=F32)

    def biased_rows(s, d):
        def rows(r):
            sr = s[r:r + ATT_ROWS]
            return sr if d is None else sr + bias_ref[d, r:r + ATT_ROWS, :]
        return rows

    def running_max(mx, x):
        return x if mx is None else jnp.maximum(mx, x)

    def softmax_pv(chain, vt, rows, mx, next_rows, p_ref):
        _, qb, c = chain
        cols = qcols(qb)
        m_old = m_ref[c, :, cols]
        m_new = jnp.maximum(m_old, jnp.max(mx, axis=0, keepdims=True))
        mx_next = None
        for r in range(0, T, ATT_ROWS):
            p_ref[r:r + ATT_ROWS, :] = jnp.exp2(rows(r) - m_new).astype(BF16)
            if next_rows is not None:
                mx_next = running_max(mx_next, next_rows(r))
        acc_ref[c, :, cols] = (jnp.exp2(m_old - m_new) * acc_ref[c, :, cols]
                               + jnp.dot(vt, p_ref[...], preferred_element_type=F32))
        m_ref[c, :, cols] = m_new
        return mx_next

    def step(kp, offset):
        vts = [jnp.concatenate([vt_ref[kp * ATT_QBLOCKS + ko], ones], axis=0)
               for ko in range(ATT_QBLOCKS)]
        kpn = jnp.minimum(kp + 1, nsteps - 1)

        def block_delta(chain):
            ko, qb, _ = chain
            return None if offset is None else offset * ATT_QBLOCKS + ko - qb

        def rows_of(i):
            delta = block_delta(chains[i])
            return biased_rows(s[i], delta + 1 if delta in (-1, 0, 1) else None)

        s = [s0_ref[i] for i in range(ATT_DEPTH)]
        rows = rows_of(0)
        mx = None
        for r in range(0, T, ATT_ROWS):
            mx = running_max(mx, rows(r))
        for i, chain in enumerate(chains):
            ko, qb, c = chain
            ahead = i + ATT_DEPTH
            if ahead < len(chains):
                s.append(scores(chains[ahead], kp))
            else:
                s0_ref[ahead - len(chains)] = scores(chains[ahead - len(chains)], kpn)
            delta = block_delta(chain)
            if c == 0 and delta == -1:
                m_ref[:, :, qcols(qb)] = m_ref[:, :, qcols(qb)] + bias_left
            if c == 0 and delta == 2:
                m_ref[:, :, qcols(qb)] = m_ref[:, :, qcols(qb)] - bias_right
            next_rows = rows_of(i + 1) if i + 1 < len(chains) else None
            mx = softmax_pv(chain, vts[ko], rows, mx, next_rows, p_ref.at[i % 2])
            rows = next_rows

    def far_step(kp, carry):
        step(kp, None)
        return carry

    nsteps = nk // ATT_QBLOCKS
    for i in range(ATT_DEPTH):
        s0_ref[i] = scores(chains[i], 0)
    lax.fori_loop(0, qi - 1, far_step, 0)
    for offset in (-1, 0, 1):
        kp = qi + offset

        @pl.when((kp >= 0) & (kp < nsteps))
        def _():
            step(jnp.clip(kp, 0, nsteps - 1), offset)

    lax.fori_loop(qi + 2, nsteps, far_step, 0)

    def normalised(c):
        return acc_ref[c, :DIFF_DV, :] / acc_ref[c, DIFF_DV:DIFF_DV + 1, :]

    ot = normalised(0) - lam * normalised(1)
    o = _rms(ot.T, ng_ref[...]) * (1.0 - lam_init)
    o_ref[...] = (o * _silu(g_ref[...])).astype(BF16)


def _diff_attn(scal, dq, dk, dvt, btab, dg, ng, lam_init):
    B, S, _ = dq.shape
    T = ATT_TILE
    TQ = ATT_QBLOCKS * T
    q_spec = pl.BlockSpec((None, TQ, DIFF_DV), lambda b, h, qi, sc: (b, qi, h))
    k_spec = pl.BlockSpec((None, S, DIFF_DV), lambda b, h, qi, sc: (b, 0, h))
    vt_spec = pl.BlockSpec((None, S // T, DIFF_DV, T), lambda b, h, qi, sc: (b, 0, h, 0))
    btab_spec = pl.BlockSpec((None, 3, 1, 2 * T), lambda b, h, qi, sc: (h, 0, 0, 0))
    ng_spec = pl.BlockSpec((1, DIFF_DV), lambda b, h, qi, sc: (0, 0))
    grid_spec = pltpu.PrefetchScalarGridSpec(
        num_scalar_prefetch=1,
        grid=(B, N_DIFF_HEADS, S // TQ),
        in_specs=[q_spec, k_spec, vt_spec, btab_spec, q_spec, ng_spec],
        out_specs=q_spec,
        scratch_shapes=[pltpu.VMEM((2, DIFF_DV, TQ), BF16),
                        pltpu.VMEM((3, T, T), F32),
                        pltpu.VMEM((2, 1, TQ), F32),
                        pltpu.VMEM((2, DIFF_DV + ONES_ROWS, TQ), F32),
                        pltpu.VMEM((ATT_DEPTH, T, T), F32),
                        pltpu.VMEM((2, T, T), BF16)],
    )
    return pl.pallas_call(
        functools.partial(_diff_attn_kernel, lam_init=lam_init),
        grid_spec=grid_spec,
        out_shape=jax.ShapeDtypeStruct((B, S, DIFF_W), BF16),
        compiler_params=_params(("parallel", "parallel", "arbitrary")),
        name="diff_attn",
    )(scal, dq, dk, dvt, btab, dg, ng)


def _mid_proj_kernel(x_ref, ro_ref, do_ref, wo_ref, g_ref, wi_ref, x1_ref, glu_ref, sg_ref):
    x1 = (x_ref[...]
          + jnp.dot(ro_ref[...], wo_ref[:RET_W, :], preferred_element_type=F32)
          + jnp.dot(do_ref[...], wo_ref[RET_W:, :], preferred_element_type=F32))
    x1_ref[...] = x1
    hb = _rms(x1, g_ref[...]).astype(BF16)
    step = 512
    for c0 in range(0, CONV_W, step):
        a = jnp.dot(hb, wi_ref[:, c0:c0 + step], preferred_element_type=F32)
        b = jnp.dot(hb, wi_ref[:, CONV_W + c0:CONV_W + c0 + step], preferred_element_type=F32)
        gate = jnp.dot(hb, wi_ref[:, 2 * CONV_W + c0:2 * CONV_W + c0 + step], preferred_element_type=F32)
        glu_ref[:, c0:c0 + step] = a * jax.nn.sigmoid(b)
        sg_ref[:, c0:c0 + step] = _silu(gate)


def _mid_proj(x, ro, do, wo, g, wi):
    B, S, D = x.shape
    tm = ROW_TILE
    row = lambda n: pl.BlockSpec((None, tm, n), lambda b, i: (b, i, 0))
    return pl.pallas_call(
        _mid_proj_kernel,
        grid=(B, S // tm),
        in_specs=[row(D), row(RET_W), row(DIFF_W), _whole(wo.shape), _whole((1, D)), _whole(wi.shape)],
        out_specs=(row(D), row(CONV_W), row(CONV_W)),
        out_shape=(jax.ShapeDtypeStruct((B, S, D), F32),
                   jax.ShapeDtypeStruct((B, S, CONV_W), F32),
                   jax.ShapeDtypeStruct((B, S, CONV_W), F32)),
        compiler_params=_params(("parallel", "parallel")),
        name="mid_proj",
    )(x, ro, do, wo, g, wi)


def _conv_out_kernel(glu_ref, prev_ref, next_ref, sg_ref, x1_ref, cw_ref, cb_ref, ng_ref, nb_ref,
                     wo_ref, fg_ref, o_ref, win_ref, y_ref):
    i = pl.program_id(1)
    nblk = pl.num_programs(1)
    TS = CONV_TILE
    ncb = CONV_W // LANES
    prev = jnp.where(i > 0, prev_ref[...], 0.0)
    nxt = jnp.where(i < nblk - 1, next_ref[...], 0.0)
    for cb in range(ncb):
        sl = slice(cb * LANES, (cb + 1) * LANES)
        win_ref[cb, 0:HALO, :] = prev[:, sl]
        win_ref[cb, HALO:HALO + TS, :] = glu_ref[:, sl]
        win_ref[cb, HALO + TS:HALO + TS + HALO, :] = nxt[:, sl]

    def chan_block(cb, carry):
        w = cw_ref[cb]
        for r0 in range(0, TS, CONV_ROWS):
            acc = jnp.zeros((CONV_ROWS, LANES), F32)
            for t in range(CONV_K):
                off = r0 + HALO - CONV_PAD + t
                acc = acc + win_ref[cb, pl.ds(off, CONV_ROWS), :] * w[t:t + 1, :]
            y_ref[cb, r0:r0 + CONV_ROWS, :] = acc
        return carry

    lax.fori_loop(0, ncb, chan_block, 0)

    y = jnp.concatenate([y_ref[cb] for cb in range(ncb)], axis=-1) + cb_ref[...]
    mu = jnp.mean(y, axis=-1, keepdims=True)
    yc = y - mu
    var = jnp.mean(yc * yc, axis=-1, keepdims=True)
    yn = yc * lax.rsqrt(var + EPS) * ng_ref[...] + nb_ref[...]
    z = (_silu(yn) * sg_ref[...]).astype(BF16)
    out = x1_ref[...] + jnp.dot(z, wo_ref[...], preferred_element_type=F32)
    o_ref[...] = _rms(out, fg_ref[...])


def _conv_out(glu, sg, x1, cw, cb, ng, nb, wo, fg):
    B, S, D = x1.shape
    TS = CONV_TILE
    hb = TS // HALO
    nh = S // HALO
    row = lambda n: pl.BlockSpec((None, TS, n), lambda b, i: (b, i, 0))
    prev_spec = pl.BlockSpec((None, HALO, CONV_W), lambda b, i: (b, jnp.maximum(i * hb - 1, 0), 0))
    next_spec = pl.BlockSpec((None, HALO, CONV_W), lambda b, i: (b, jnp.minimum((i + 1) * hb, nh - 1), 0))
    ncb = CONV_W // LANES
    return pl.pallas_call(
        _conv_out_kernel,
        grid=(B, S // TS),
        in_specs=[row(CONV_W), prev_spec, next_spec, row(CONV_W), row(D),
                  _whole(cw.shape), _whole((1, CONV_W)), _whole((1, CONV_W)), _whole((1, CONV_W)),
                  _whole(wo.shape), _whole((1, D))],
        out_specs=row(D),
        out_shape=jax.ShapeDtypeStruct((B, S, D), F32),
        scratch_shapes=[pltpu.VMEM((ncb, TS + 2 * HALO, LANES), F32),
                        pltpu.VMEM((ncb, TS, LANES), F32)],
        compiler_params=_params(("parallel", "parallel")),
        name="conv_out",
    )(glu, glu, glu, sg, x1, cw, cb, ng, nb, wo, fg)


def _t5_bucket_np(rel):
    nb = T5_BUCKETS // 2
    max_exact = nb // 2
    ret = (rel > 0).astype(np.int64) * nb
    n = np.abs(rel)
    nf = np.maximum(n, 1).astype(np.float64)
    large = max_exact + (np.log(nf / max_exact) / math.log(T5_MAX_DIST / max_exact)
                         * (nb - max_exact)).astype(np.int64)
    large = np.minimum(large, nb - 1)
    return ret + np.where(n < max_exact, n, large)


def _rope_tables(S):
    half = RET_DK // 2
    inv = 1.0 / (ROPE_BASE ** (jnp.arange(0, RET_DK, 2, dtype=F32) / RET_DK))
    ang = jnp.arange(S).astype(F32)[:, None] * inv[None, :]
    cos = jnp.cos(ang)
    sin = jnp.sin(ang)
    assert cos.shape == (S, half)
    return jnp.concatenate([cos, cos], axis=-1), jnp.concatenate([-sin, sin], axis=-1)


def _trunk(x, p):
    B, S, D = x.shape
    assert D == D_MODEL and S % ATT_TILE == 0 and S % RET_CHUNK == 0 and S % ROW_TILE == 0

    cos, sin = _rope_tables(S)
    rq, rk, rv, rg, dq, dk, dvt, dg = _even_inproj(x, p["g0"], p["w_in_mix"], p["wvt"], cos, sin)
    ro = _retention(p["log_g"], rq, rk, rv, rg)
    do = _diff_attn(p["attn_scal"], dq, dk, dvt, p["btab"], dg, p["diff_norm_g"], p["lam_init"])
    x1, glu, sg = _mid_proj(x, ro, do, p["w_out_mix"], p["g1"], p["w_in_conv"])
    return _conv_out(glu, sg, x1, p["conv_w"], p["conv_b"], p["conv_norm_g"], p["conv_norm_b"],
                     p["w_out_conv"], p["final_g"])


def kernel(x_prompt, x_sample, norm_g, final_g, rel_bias, w_in_mix, ret_decay, lam_q1, lam_k1, lam_q2,
           lam_k2, diff_norm_g, w_out_mix, w_in_conv, conv_w, conv_b, conv_norm_g, conv_norm_b,
           w_out_conv):
    assert norm_g.shape[0] == 2 and w_in_mix.shape[0] == 1 and w_in_conv.shape[0] == 1
    lam_init = 0.8 - 0.6 * math.exp(-0.3 * 0)
    lam = (jnp.exp(jnp.sum(lam_q1[0].astype(F32) * lam_k1[0].astype(F32)))
           - jnp.exp(jnp.sum(lam_q2[0].astype(F32) * lam_k2[0].astype(F32))) + lam_init)
    T = ATT_TILE
    i = np.arange(2 * T)
    idx = np.stack([_t5_bucket_np(np.where(i <= T, off * T - i, off * T + 2 * T - i))
                    for off in (-1, 0, 1)]).astype(np.int32)
    bias2 = rel_bias.astype(F32) * LOG2E
    btab = jnp.take(bias2.T, jnp.asarray(idx), axis=1).reshape(N_DIFF_HEADS, 3, 1, 2 * T)
    far_left = int(_t5_bucket_np(np.array(-T5_MAX_DIST)))
    far_right = int(_t5_bucket_np(np.array(T5_MAX_DIST)))
    attn_scal = jnp.concatenate([lam.reshape(1), bias2[far_left], bias2[far_right]])
    wim = w_in_mix[0]
    ncb = CONV_W // LANES
    p = dict(
        g0=norm_g[0].reshape(1, D_MODEL), g1=norm_g[1].reshape(1, D_MODEL),
        final_g=final_g.reshape(1, D_MODEL),
        w_in_mix=wim.astype(BF16),
        wvt=wim[:, _C_DV:_C_DV + DIFF_W].T.astype(BF16),
        log_g=(-jnp.exp(ret_decay[0].astype(F32))).reshape(-1),
        attn_scal=attn_scal, btab=btab, lam_init=lam_init,
        diff_norm_g=diff_norm_g[0].reshape(1, DIFF_DV),
        w_out_mix=w_out_mix[0].astype(BF16),
        w_in_conv=w_in_conv[0].astype(BF16),
        conv_w=conv_w[0].reshape(CONV_K, ncb, LANES).transpose(1, 0, 2),
        conv_b=conv_b[0].reshape(1, CONV_W),
        conv_norm_g=conv_norm_g[0].reshape(1, CONV_W),
        conv_norm_b=conv_norm_b[0].reshape(1, CONV_W),
        w_out_conv=w_out_conv[0].astype(BF16),
    )
    return (_trunk(x_prompt, p), _trunk(x_sample, p))
```

```python
import functools
import math

import numpy as np
import jax
import jax.numpy as jnp
from jax import lax
from jax.experimental import pallas as pl
from jax.experimental.pallas import tpu as pltpu

F32 = jnp.float32
BF16 = jnp.bfloat16

D_MODEL = 1024
N_RET_HEADS = 4
RET_DK = 128
RET_DV = 256
N_DIFF_HEADS = 8
DIFF_DH = 64
DIFF_DV = 2 * DIFF_DH
RET_W = N_RET_HEADS * RET_DV
DIFF_W = N_DIFF_HEADS * DIFF_DV
CONV_W = 2 * D_MODEL
CONV_K = 31
CONV_PAD = CONV_K // 2
T5_BUCKETS = 32
T5_MAX_DIST = 128
ROPE_BASE = 10000.0
EPS = 1e-6

_C_RQ = 0
_C_RK = _C_RQ + N_RET_HEADS * RET_DK
_C_RV = _C_RK + N_RET_HEADS * RET_DK
_C_RG = _C_RV + RET_W
_C_DQ = _C_RG + RET_W
_C_DK = _C_DQ + DIFF_W
_C_DV = _C_DK + DIFF_W
_C_DG = _C_DV + DIFF_W
EVEN_IN_W = _C_DG + DIFF_W

LANES = 128
HALO = 16
VMEM_LIMIT = 56 * 1024 * 1024

ROW_TILE = 256
RET_CHUNK = 256
RET_BLOCK = 512
ATT_TILE = 256
ATT_QBLOCKS = 4
ATT_ROWS = 32
ATT_DEPTH = 4
ONES_ROWS = 16
LOG2E = math.log2(math.e)
CONV_TILE = 256
CONV_ROWS = 64

M_INIT = -1e30


def _rms(x, g):
    return x * lax.rsqrt(jnp.mean(x * x, axis=-1, keepdims=True) + EPS) * g


def _silu(x):
    return x * jax.nn.sigmoid(x)


def _params(sem):
    return pltpu.CompilerParams(dimension_semantics=sem, vmem_limit_bytes=VMEM_LIMIT)


def _whole(shape):
    nd = len(shape)
    return pl.BlockSpec(shape, lambda *_: (0,) * nd, pipeline_mode=pl.Buffered(1))


def _even_inproj_kernel(x_ref, g_ref, w_ref, wvt_ref, cos_ref, sin_ref,
                        rq_ref, rk_ref, rv_ref, rg_ref, dq_ref, dk_ref, dvt_ref, dg_ref):
    hb = _rms(x_ref[...], g_ref[...]).astype(BF16)

    def proj(c0, n):
        return jnp.dot(hb, w_ref[:, c0:c0 + n], preferred_element_type=F32)

    cos = cos_ref[...]
    sin = sin_ref[...]

    def rope(u):
        return u * cos + pltpu.roll(u, RET_DK // 2, 1) * sin

    uq = proj(_C_RQ, N_RET_HEADS * RET_DK)
    uk = proj(_C_RK, N_RET_HEADS * RET_DK)
    for j in range(N_RET_HEADS):
        sl = slice(j * RET_DK, (j + 1) * RET_DK)
        rq_ref[:, sl] = rope(uq[:, sl]).astype(BF16)
        rk_ref[:, sl] = (rope(uk[:, sl]) * (RET_DK ** -0.5)).astype(BF16)
    rv_ref[...] = proj(_C_RV, RET_W).astype(BF16)
    rg_ref[...] = proj(_C_RG, RET_W)
    dq_ref[...] = (proj(_C_DQ, DIFF_W) * (DIFF_DH ** -0.5 * LOG2E)).astype(BF16)
    dk_ref[...] = proj(_C_DK, DIFF_W).astype(BF16)
    dg_ref[...] = proj(_C_DG, DIFF_W)
    dvt_ref[...] = lax.dot_general(wvt_ref[...], hb, (((1,), (1,)), ((), ())),
                                   preferred_element_type=F32).astype(BF16)


def _even_inproj(x, g, w, wvt, cos, sin):
    B, S, D = x.shape
    tm = ROW_TILE
    row = lambda n: pl.BlockSpec((None, tm, n), lambda b, i: (b, i, 0))
    tab = pl.BlockSpec((tm, LANES), lambda b, i: (i, 0))
    out_shape = (
        jax.ShapeDtypeStruct((B, S, N_RET_HEADS * RET_DK), BF16),
        jax.ShapeDtypeStruct((B, S, N_RET_HEADS * RET_DK), BF16),
        jax.ShapeDtypeStruct((B, S, RET_W), BF16),
        jax.ShapeDtypeStruct((B, S, RET_W), F32),
        jax.ShapeDtypeStruct((B, S, DIFF_W), BF16),
        jax.ShapeDtypeStruct((B, S, DIFF_W), BF16),
        jax.ShapeDtypeStruct((B, S // ATT_TILE, DIFF_W, ATT_TILE), BF16),
        jax.ShapeDtypeStruct((B, S, DIFF_W), F32),
    )
    per_tile = ATT_TILE // tm
    out_specs = (row(N_RET_HEADS * RET_DK), row(N_RET_HEADS * RET_DK), row(RET_W), row(RET_W),
                 row(DIFF_W), row(DIFF_W),
                 pl.BlockSpec((None, None, DIFF_W, tm), lambda b, i: (b, i // per_tile, 0, i % per_tile)),
                 row(DIFF_W))
    return pl.pallas_call(
        _even_inproj_kernel,
        grid=(B, S // tm),
        in_specs=[row(D), _whole((1, D)), _whole(w.shape), _whole(wvt.shape), tab, tab],
        out_specs=out_specs,
        out_shape=out_shape,
        compiler_params=_params(("parallel", "parallel")),
        name="even_inproj",
    )(x, g, w, wvt, cos, sin)


def _retention_kernel(lg_ref, q_ref, k_ref, v_ref, g_ref, o_ref, lst_ref, st_ref, dmat_ref):
    n = pl.program_id(1)
    nblk = pl.num_programs(1) // 2
    C = RET_CHUNK
    cpb = RET_BLOCK // C
    heads = range(N_RET_HEADS)
    lf = [lg_ref[h] for h in heads]
    lb = [lg_ref[N_RET_HEADS + h] for h in heads]
    pos = lax.broadcasted_iota(jnp.int32, (C, 1), 0).astype(F32)

    def kcols(h):
        return slice(h * RET_DK, (h + 1) * RET_DK)

    def vcols(h):
        return slice(h * RET_DV, (h + 1) * RET_DV)

    def decay(x):
        return jnp.exp(jnp.full((1, RET_DV), x, F32))

    def kt_v(h, rows, w_col):
        kw = (k_ref[rows, kcols(h)].astype(F32) * w_col).T.astype(BF16)
        return jnp.dot(kw, v_ref[rows, vcols(h)], preferred_element_type=F32)

    @pl.when(n == 0)
    def _():
        r = lax.broadcasted_iota(jnp.int32, (C, C), 0)
        c = lax.broadcasted_iota(jnp.int32, (C, C), 1)
        d = (r - c).astype(F32)
        for h in heads:
            dmat_ref[h] = jnp.where(d >= 0, jnp.exp(lf[h] * jnp.maximum(d, 0.0)),
                                    jnp.exp(lb[h] * jnp.maximum(-d, 0.0)))

    @pl.when((n == 0) | (n == nblk))
    def _():
        st_ref[...] = jnp.zeros_like(st_ref)

    @pl.when(n < nblk)
    def _():
        blk = nblk - 1 - n
        for cc in reversed(range(cpb)):
            rows = slice(cc * C, (cc + 1) * C)
            chunk = blk * cpb + cc
            upd = [kt_v(h, rows, jnp.exp(lb[h] * pos)) for h in heads]
            for h in heads:
                lst_ref[chunk, h] = st_ref[h].astype(BF16)
                st_ref[h] = decay(lb[h] * C) * st_ref[h] + upd[h]

    @pl.when(n >= nblk)
    def _():
        blk = n - nblk
        for cc in range(cpb):
            rows = slice(cc * C, (cc + 1) * C)
            chunk = blk * cpb + cc
            q = [q_ref[rows, kcols(h)] for h in heads]
            s = [lax.dot_general(q[h], k_ref[rows, kcols(h)], (((1,), (1,)), ((), ())),
                                 preferred_element_type=F32) for h in heads]
            qf = [(q[h].astype(F32) * jnp.exp(lf[h] * (pos + 1.0))).astype(BF16) for h in heads]
            qb = [(q[h].astype(F32) * jnp.exp(lb[h] * (C - pos))).astype(BF16) for h in heads]
            sd = [(s[h] * dmat_ref[h]).astype(BF16) for h in heads]
            o = [jnp.dot(sd[h], v_ref[rows, vcols(h)], preferred_element_type=F32)
                 + jnp.dot(qf[h], st_ref[h].astype(BF16), preferred_element_type=F32)
                 + jnp.dot(qb[h], lst_ref[chunk, h], preferred_element_type=F32) for h in heads]
            upd = [kt_v(h, rows, jnp.exp(lf[h] * (C - 1.0 - pos))) for h in heads]
            for h in heads:
                st_ref[h] = decay(lf[h] * C) * st_ref[h] + upd[h]
                oh = o[h] * lax.rsqrt(jnp.mean(o[h] * o[h], axis=-1, keepdims=True) + EPS)
                o_ref[rows, vcols(h)] = (oh * _silu(g_ref[rows, vcols(h)])).astype(BF16)


def _retention(log_g, rq, rk, rv, rg):
    B, S, _ = rq.shape
    RB = RET_BLOCK
    NB = S // RB
    KW = N_RET_HEADS * RET_DK

    def both(n):
        return jnp.where(n < NB, NB - 1 - n, n - NB)

    def fwd(n):
        return jnp.maximum(n - NB, 0)

    q_spec = pl.BlockSpec((None, RB, KW), lambda b, n, lg: (b, fwd(n), 0))
    k_spec = pl.BlockSpec((None, RB, KW), lambda b, n, lg: (b, both(n), 0))
    v_spec = pl.BlockSpec((None, RB, RET_W), lambda b, n, lg: (b, both(n), 0))
    fwd_spec = pl.BlockSpec((None, RB, RET_W), lambda b, n, lg: (b, fwd(n), 0))
    grid_spec = pltpu.PrefetchScalarGridSpec(
        num_scalar_prefetch=1,
        grid=(B, 2 * NB),
        in_specs=[q_spec, k_spec, v_spec, fwd_spec],
        out_specs=fwd_spec,
        scratch_shapes=[pltpu.VMEM((S // RET_CHUNK, N_RET_HEADS, RET_DK, RET_DV), BF16),
                        pltpu.VMEM((N_RET_HEADS, RET_DK, RET_DV), F32),
                        pltpu.VMEM((N_RET_HEADS, RET_CHUNK, RET_CHUNK), F32)],
    )
    return pl.pallas_call(
        _retention_kernel,
        grid_spec=grid_spec,
        out_shape=jax.ShapeDtypeStruct((B, S, RET_W), BF16),
        compiler_params=_params(("parallel", "arbitrary")),
        name="retention",
    )(log_g, rq, rk, rv, rg)


def _diff_attn_kernel(sc_ref, q_ref, k_ref, vt_ref, btab_ref, g_ref, ng_ref, o_ref,
                      qz_ref, bias_ref, m_ref, acc_ref, s0_ref, p_ref, *, lam_init):
    h = pl.program_id(1)
    qi = pl.program_id(2)
    T = ATT_TILE
    nk = vt_ref.shape[0]
    lam = sc_ref[0]
    bias_left = sc_ref[1 + h]
    bias_right = sc_ref[1 + N_DIFF_HEADS + h]

    @pl.when(qi == 0)
    def _():
        for d in range(3):
            x = jnp.broadcast_to(btab_ref[d], (T, 2 * T))
            bias_ref[d] = pltpu.roll(x, 0, 1, stride=1, stride_axis=0)[:, :T]

    qt = q_ref[...].astype(F32).T
    half = lax.broadcasted_iota(jnp.int32, qt.shape, 0) < DIFF_DH
    qz_ref[0] = jnp.where(half, qt, 0.0).astype(BF16)
    qz_ref[1] = jnp.where(half, 0.0, qt).astype(BF16)
    m_ref[...] = jnp.full_like(m_ref, M_INIT)
    acc_ref[...] = jnp.zeros_like(acc_ref)
    ones = jnp.ones((ONES_ROWS, T), BF16)

    chains = [(ko, qb, c) for ko in range(ATT_QBLOCKS) for qb in range(ATT_QBLOCKS) for c in range(2)]

    def qcols(qb):
        return slice(qb * T, (qb + 1) * T)

    def scores(chain, kp):
        ko, qb, c = chain
        row0 = pl.multiple_of((kp * ATT_QBLOCKS + ko) * T, T)
        k = k_ref[pl.ds(row0, T), :]
        return jnp.dot(k, qz_ref[c, :, qcols(qb)], preferred_element_type=F32)

    def biased_rows(s, d):
        def rows(r):
            sr = s[r:r + ATT_ROWS]
            return sr if d is None else sr + bias_ref[d, r:r + ATT_ROWS, :]
        return rows

    def running_max(mx, x):
        return x if mx is None else jnp.maximum(mx, x)

    def softmax_pv(chain, vt, rows, mx, next_rows, p_ref):
        _, qb, c = chain
        cols = qcols(qb)
        m_old = m_ref[c, :, cols]
        m_new = jnp.maximum(m_old, jnp.max(mx, axis=0, keepdims=True))
        mx_next = None
        for r in range(0, T, ATT_ROWS):
            p_ref[r:r + ATT_ROWS, :] = jnp.exp2(rows(r) - m_new).astype(BF16)
            if next_rows is not None:
                mx_next = running_max(mx_next, next_rows(r))
        acc_ref[c, :, cols] = (jnp.exp2(m_old - m_new) * acc_ref[c, :, cols]
                               + jnp.dot(vt, p_ref[...], preferred_element_type=F32))
        m_ref[c, :, cols] = m_new
        return mx_next

    def step(kp, offset):
        vts = [jnp.concatenate([vt_ref[kp * ATT_QBLOCKS + ko], ones], axis=0)
               for ko in range(ATT_QBLOCKS)]
        kpn = jnp.minimum(kp + 1, nsteps - 1)

        def block_delta(chain):
            ko, qb, _ = chain
            return None if offset is None else offset * ATT_QBLOCKS + ko - qb

        def rows_of(i):
            delta = block_delta(chains[i])
            return biased_rows(s[i], delta + 1 if delta in (-1, 0, 1) else None)

        s = [s0_ref[i] for i in range(ATT_DEPTH)]
        rows = rows_of(0)
        mx = None
        for r in range(0, T, ATT_ROWS):
            mx = running_max(mx, rows(r))
        for i, chain in enumerate(chains):
            ko, qb, c = chain
            ahead = i + ATT_DEPTH
            if ahead < len(chains):
                s.append(scores(chains[ahead], kp))
            else:
                s0_ref[ahead - len(chains)] = scores(chains[ahead - len(chains)], kpn)
            delta = block_delta(chain)
            if c == 0 and delta == -1:
                m_ref[:, :, qcols(qb)] = m_ref[:, :, qcols(qb)] + bias_left
            if c == 0 and delta == 2:
                m_ref[:, :, qcols(qb)] = m_ref[:, :, qcols(qb)] - bias_right
            next_rows = rows_of(i + 1) if i + 1 < len(chains) else None
            mx = softmax_pv(chain, vts[ko], rows, mx, next_rows, p_ref.at[i % 2])
            rows = next_rows

    def far_step(kp, carry):
        step(kp, None)
        return carry

    nsteps = nk // ATT_QBLOCKS
    for i in range(ATT_DEPTH):
        s0_ref[i] = scores(chains[i], 0)
    lax.fori_loop(0, qi - 1, far_step, 0)
    for offset in (-1, 0, 1):
        kp = qi + offset

        @pl.when((kp >= 0) & (kp < nsteps))
        def _():
            step(jnp.clip(kp, 0, nsteps - 1), offset)

    lax.fori_loop(qi + 2, nsteps, far_step, 0)

    def normalised(c):
        return acc_ref[c, :DIFF_DV, :] / acc_ref[c, DIFF_DV:DIFF_DV + 1, :]

    ot = normalised(0) - lam * normalised(1)
    o = _rms(ot.T, ng_ref[...]) * (1.0 - lam_init)
    o_ref[...] = (o * _silu(g_ref[...])).astype(BF16)


def _diff_attn(scal, dq, dk, dvt, btab, dg, ng, lam_init):
    B, S, _ = dq.shape
    T = ATT_TILE
    TQ = ATT_QBLOCKS * T
    q_spec = pl.BlockSpec((None, TQ, DIFF_DV), lambda b, h, qi, sc: (b, qi, h))
    k_spec = pl.BlockSpec((None, S, DIFF_DV), lambda b, h, qi, sc: (b, 0, h))
    vt_spec = pl.BlockSpec((None, S // T, DIFF_DV, T), lambda b, h, qi, sc: (b, 0, h, 0))
    btab_spec = pl.BlockSpec((None, 3, 1, 2 * T), lambda b, h, qi, sc: (h, 0, 0, 0))
    ng_spec = pl.BlockSpec((1, DIFF_DV), lambda b, h, qi, sc: (0, 0))
    grid_spec = pltpu.PrefetchScalarGridSpec(
        num_scalar_prefetch=1,
        grid=(B, N_DIFF_HEADS, S // TQ),
        in_specs=[q_spec, k_spec, vt_spec, btab_spec, q_spec, ng_spec],
        out_specs=q_spec,
        scratch_shapes=[pltpu.VMEM((2, DIFF_DV, TQ), BF16),
                        pltpu.VMEM((3, T, T), F32),
                        pltpu.VMEM((2, 1, TQ), F32),
                        pltpu.VMEM((2, DIFF_DV + ONES_ROWS, TQ), F32),
                        pltpu.VMEM((ATT_DEPTH, T, T), F32),
                        pltpu.VMEM((2, T, T), BF16)],
    )
    return pl.pallas_call(
        functools.partial(_diff_attn_kernel, lam_init=lam_init),
        grid_spec=grid_spec,
        out_shape=jax.ShapeDtypeStruct((B, S, DIFF_W), BF16),
        compiler_params=_params(("parallel", "parallel", "arbitrary")),
        name="diff_attn",
    )(scal, dq, dk, dvt, btab, dg, ng)


def _mid_proj_kernel(x_ref, ro_ref, do_ref, wo_ref, g_ref, wi_ref, x1_ref, glu_ref, sg_ref):
    x1 = (x_ref[...]
          + jnp.dot(ro_ref[...], wo_ref[:RET_W, :], preferred_element_type=F32)
          + jnp.dot(do_ref[...], wo_ref[RET_W:, :], preferred_element_type=F32))
    x1_ref[...] = x1
    hb = _rms(x1, g_ref[...]).astype(BF16)
    step = 512
    for c0 in range(0, CONV_W, step):
        a = jnp.dot(hb, wi_ref[:, c0:c0 + step], preferred_element_type=F32)
        b = jnp.dot(hb, wi_ref[:, CONV_W + c0:CONV_W + c0 + step], preferred_element_type=F32)
        gate = jnp.dot(hb, wi_ref[:, 2 * CONV_W + c0:2 * CONV_W + c0 + step], preferred_element_type=F32)
        glu_ref[:, c0:c0 + step] = a * jax.nn.sigmoid(b)
        sg_ref[:, c0:c0 + step] = _silu(gate)


def _mid_proj(x, ro, do, wo, g, wi):
    B, S, D = x.shape
    tm = ROW_TILE
    row = lambda n: pl.BlockSpec((None, tm, n), lambda b, i: (b, i, 0))
    return pl.pallas_call(
        _mid_proj_kernel,
        grid=(B, S // tm),
        in_specs=[row(D), row(RET_W), row(DIFF_W), _whole(wo.shape), _whole((1, D)), _whole(wi.shape)],
        out_specs=(row(D), row(CONV_W), row(CONV_W)),
        out_shape=(jax.ShapeDtypeStruct((B, S, D), F32),
                   jax.ShapeDtypeStruct((B, S, CONV_W), F32),
                   jax.ShapeDtypeStruct((B, S, CONV_W), F32)),
        compiler_params=_params(("parallel", "parallel")),
        name="mid_proj",
    )(x, ro, do, wo, g, wi)


def _conv_out_kernel(glu_ref, prev_ref, next_ref, sg_ref, x1_ref, cw_ref, cb_ref, ng_ref, nb_ref,
                     wo_ref, fg_ref, o_ref, win_ref, y_ref):
    i = pl.program_id(1)
    nblk = pl.num_programs(1)
    TS = CONV_TILE
    ncb = CONV_W // LANES
    prev = jnp.where(i > 0, prev_ref[...], 0.0)
    nxt = jnp.where(i < nblk - 1, next_ref[...], 0.0)
    for cb in range(ncb):
        sl = slice(cb * LANES, (cb + 1) * LANES)
        win_ref[cb, 0:HALO, :] = prev[:, sl]
        win_ref[cb, HALO:HALO + TS, :] = glu_ref[:, sl]
        win_ref[cb, HALO + TS:HALO + TS + HALO, :] = nxt[:, sl]

    def chan_block(cb, carry):
        w = cw_ref[cb]
        for r0 in range(0, TS, CONV_ROWS):
            acc = jnp.zeros((CONV_ROWS, LANES), F32)
            for t in range(CONV_K):
                off = r0 + HALO - CONV_PAD + t
                acc = acc + win_ref[cb, pl.ds(off, CONV_ROWS), :] * w[t:t + 1, :]
            y_ref[cb, r0:r0 + CONV_ROWS, :] = acc
        return carry

    lax.fori_loop(0, ncb, chan_block, 0)

    y = jnp.concatenate([y_ref[cb] for cb in range(ncb)], axis=-1) + cb_ref[...]
    mu = jnp.mean(y, axis=-1, keepdims=True)
    yc = y - mu
    var = jnp.mean(yc * yc, axis=-1, keepdims=True)
    yn = yc * lax.rsqrt(var + EPS) * ng_ref[...] + nb_ref[...]
    z = (_silu(yn) * sg_ref[...]).astype(BF16)
    out = x1_ref[...] + jnp.dot(z, wo_ref[...], preferred_element_type=F32)
    o_ref[...] = _rms(out, fg_ref[...])


def _conv_out(glu, sg, x1, cw, cb, ng, nb, wo, fg):
    B, S, D = x1.shape
    TS = CONV_TILE
    hb = TS // HALO
    nh = S // HALO
    row = lambda n: pl.BlockSpec((None, TS, n), lambda b, i: (b, i, 0))
    prev_spec = pl.BlockSpec((None, HALO, CONV_W), lambda b, i: (b, jnp.maximum(i * hb - 1, 0), 0))
    next_spec = pl.BlockSpec((None, HALO, CONV_W), lambda b, i: (b, jnp.minimum((i + 1) * hb, nh - 1), 0))
    ncb = CONV_W // LANES
    return pl.pallas_call(
        _conv_out_kernel,
        grid=(B, S // TS),
        in_specs=[row(CONV_W), prev_spec, next_spec, row(CONV_W), row(D),
                  _whole(cw.shape), _whole((1, CONV_W)), _whole((1, CONV_W)), _whole((1, CONV_W)),
                  _whole(wo.shape), _whole((1, D))],
        out_specs=row(D),
        out_shape=jax.ShapeDtypeStruct((B, S, D), F32),
        scratch_shapes=[pltpu.VMEM((ncb, TS + 2 * HALO, LANES), F32),
                        pltpu.VMEM((ncb, TS, LANES), F32)],
        compiler_params=_params(("parallel", "parallel")),
        name="conv_out",
    )(glu, glu, glu, sg, x1, cw, cb, ng, nb, wo, fg)


def _t5_bucket_np(rel):
    nb = T5_BUCKETS // 2
    max_exact = nb // 2
    ret = (rel > 0).astype(np.int64) * nb
    n = np.abs(rel)
    nf = np.maximum(n, 1).astype(np.float64)
    large = max_exact + (np.log(nf / max_exact) / math.log(T5_MAX_DIST / max_exact)
                         * (nb - max_exact)).astype(np.int64)
    large = np.minimum(large, nb - 1)
    return ret + np.where(n < max_exact, n, large)


def _rope_tables(S):
    half = RET_DK // 2
    inv = 1.0 / (ROPE_BASE ** (jnp.arange(0, RET_DK, 2, dtype=F32) / RET_DK))
    ang = jnp.arange(S).astype(F32)[:, None] * inv[None, :]
    cos = jnp.cos(ang)
    sin = jnp.sin(ang)
    assert cos.shape == (S, half)
    return jnp.concatenate([cos, cos], axis=-1), jnp.concatenate([-sin, sin], axis=-1)


def _trunk(x, p):
    B, S, D = x.shape
    assert D == D_MODEL and S % ATT_TILE == 0 and S % RET_CHUNK == 0 and S % ROW_TILE == 0

    cos, sin = _rope_tables(S)
    rq, rk, rv, rg, dq, dk, dvt, dg = _even_inproj(x, p["g0"], p["w_in_mix"], p["wvt"], cos, sin)
    ro = _retention(p["log_g"], rq, rk, rv, rg)
    do = _diff_attn(p["attn_scal"], dq, dk, dvt, p["btab"], dg, p["diff_norm_g"], p["lam_init"])
    x1, glu, sg = _mid_proj(x, ro, do, p["w_out_mix"], p["g1"], p["w_in_conv"])
    return _conv_out(glu, sg, x1, p["conv_w"], p["conv_b"], p["conv_norm_g"], p["conv_norm_b"],
                     p["w_out_conv"], p["final_g"])


def kernel(x_prompt, x_sample, norm_g, final_g, rel_bias, w_in_mix, ret_decay, lam_q1, lam_k1, lam_q2,
           lam_k2, diff_norm_g, w_out_mix, w_in_conv, conv_w, conv_b, conv_norm_g, conv_norm_b,
           w_out_conv):
    assert norm_g.shape[0] == 2 and w_in_mix.shape[0] == 1 and w_in_conv.shape[0] == 1
    lam_init = 0.8 - 0.6 * math.exp(-0.3 * 0)
    lam = (jnp.exp(jnp.sum(lam_q1[0].astype(F32) * lam_k1[0].astype(F32)))
           - jnp.exp(jnp.sum(lam_q2[0].astype(F32) * lam_k2[0].astype(F32))) + lam_init)
    T = ATT_TILE
    i = np.arange(2 * T)
    idx = np.stack([_t5_bucket_np(np.where(i <= T, off * T - i, off * T + 2 * T - i))
                    for off in (-1, 0, 1)]).astype(np.int32)
    bias2 = rel_bias.astype(F32) * LOG2E
    btab = jnp.take(bias2.T, jnp.asarray(idx), axis=1).reshape(N_DIFF_HEADS, 3, 1, 2 * T)
    far_left = int(_t5_bucket_np(np.array(-T5_MAX_DIST)))
    far_right = int(_t5_bucket_np(np.array(T5_MAX_DIST)))
    attn_scal = jnp.concatenate([lam.reshape(1), bias2[far_left], bias2[far_right]])
    wim = w_in_mix[0]
    ncb = CONV_W // LANES
    p = dict(
        g0=norm_g[0].reshape(1, D_MODEL), g1=norm_g[1].reshape(1, D_MODEL),
        final_g=final_g.reshape(1, D_MODEL),
        w_in_mix=wim.astype(BF16),
        wvt=wim[:, _C_DV:_C_DV + DIFF_W].T.astype(BF16),
        log_g=(-jnp.exp(ret_decay[0].astype(F32))).reshape(-1),
        attn_scal=attn_scal, btab=btab, lam_init=lam_init,
        diff_norm_g=diff_norm_g[0].reshape(1, DIFF_DV),
        w_out_mix=w_out_mix[0].astype(BF16),
        w_in_conv=w_in_conv[0].astype(BF16),
        conv_w=conv_w[0].reshape(CONV_K, ncb, LANES).transpose(1, 0, 2),
        conv_b=conv_b[0].reshape(1, CONV_W),
        conv_norm_g=conv_norm_g[0].reshape(1, CONV_W),
        conv_norm_b=conv_norm_b[0].reshape(1, CONV_W),
        w_out_conv=w_out_conv[0].astype(BF16),
    )
    return (_trunk(x_prompt, p), _trunk(x_sample, p))
```

```python
import functools
import math

import numpy as np
import jax
import jax.numpy as jnp
from jax import lax
from jax.experimental import pallas as pl
from jax.experimental.pallas import tpu as pltpu

F32 = jnp.float32
BF16 = jnp.bfloat16

D_MODEL = 1024
N_RET_HEADS = 4
RET_DK = 128
RET_DV = 256
N_DIFF_HEADS = 8
DIFF_DH = 64
DIFF_DV = 2 * DIFF_DH
RET_W = N_RET_HEADS * RET_DV
DIFF_W = N_DIFF_HEADS * DIFF_DV
CONV_W = 2 * D_MODEL
CONV_K = 31
CONV_PAD = CONV_K // 2
T5_BUCKETS = 32
T5_MAX_DIST = 128
ROPE_BASE = 10000.0
EPS = 1e-6

_C_RQ = 0
_C_RK = _C_RQ + N_RET_HEADS * RET_DK
_C_RV = _C_RK + N_RET_HEADS * RET_DK
_C_RG = _C_RV + RET_W
_C_DQ = _C_RG + RET_W
_C_DK = _C_DQ + DIFF_W
_C_DV = _C_DK + DIFF_W
_C_DG = _C_DV + DIFF_W
EVEN_IN_W = _C_DG + DIFF_W

LANES = 128
HALO = 16
VMEM_LIMIT = 56 * 1024 * 1024

ROW_TILE = 256
MID_TILE = 512
RET_CHUNK = 256
RET_BLOCK = 1024
ATT_TILE = 256
ATT_QBLOCKS = 4
ATT_ROWS = 32
ATT_DEPTH = 4
ONES_ROWS = 16
LOG2E = math.log2(math.e)
CONV_TILE = 512
CONV_ROWS = 64

M_INIT = -1e30


def _rms(x, g):
    return x * lax.rsqrt(jnp.mean(x * x, axis=-1, keepdims=True) + EPS) * g


def _silu(x):
    return x * jax.nn.sigmoid(x)


def _params(sem):
    return pltpu.CompilerParams(dimension_semantics=sem, vmem_limit_bytes=VMEM_LIMIT)


def _whole(shape):
    nd = len(shape)
    return pl.BlockSpec(shape, lambda *_: (0,) * nd, pipeline_mode=pl.Buffered(1))


def _even_inproj_kernel(x_ref, g_ref, w_ref, wvt_ref, cos_ref, sin_ref,
                        rq_ref, rk_ref, rv_ref, rg_ref, dq_ref, dk_ref, dvt_ref, dg_ref):
    hb = _rms(x_ref[...], g_ref[...]).astype(BF16)

    def proj(c0, n):
        return jnp.dot(hb, w_ref[:, c0:c0 + n], preferred_element_type=F32)

    cos = cos_ref[...]
    sin = sin_ref[...]

    def rope(u):
        return u * cos + pltpu.roll(u, RET_DK // 2, 1) * sin

    uq = proj(_C_RQ, N_RET_HEADS * RET_DK)
    uk = proj(_C_RK, N_RET_HEADS * RET_DK)
    for j in range(N_RET_HEADS):
        sl = slice(j * RET_DK, (j + 1) * RET_DK)
        rq_ref[:, sl] = rope(uq[:, sl]).astype(BF16)
        rk_ref[:, sl] = (rope(uk[:, sl]) * (RET_DK ** -0.5)).astype(BF16)
    rv_ref[...] = proj(_C_RV, RET_W).astype(BF16)
    rg_ref[...] = proj(_C_RG, RET_W)
    dq_ref[...] = (proj(_C_DQ, DIFF_W) * (DIFF_DH ** -0.5 * LOG2E)).astype(BF16)
    dk_ref[...] = proj(_C_DK, DIFF_W).astype(BF16)
    dg_ref[...] = proj(_C_DG, DIFF_W)
    dvt_ref[...] = lax.dot_general(wvt_ref[...], hb, (((1,), (1,)), ((), ())),
                                   preferred_element_type=F32).astype(BF16)


def _even_inproj(x, g, w, wvt, cos, sin):
    B, S, D = x.shape
    tm = ROW_TILE
    row = lambda n: pl.BlockSpec((None, tm, n), lambda b, i: (b, i, 0))
    tab = pl.BlockSpec((tm, LANES), lambda b, i: (i, 0))
    out_shape = (
        jax.ShapeDtypeStruct((B, S, N_RET_HEADS * RET_DK), BF16),
        jax.ShapeDtypeStruct((B, S, N_RET_HEADS * RET_DK), BF16),
        jax.ShapeDtypeStruct((B, S, RET_W), BF16),
        jax.ShapeDtypeStruct((B, S, RET_W), F32),
        jax.ShapeDtypeStruct((B, S, DIFF_W), BF16),
        jax.ShapeDtypeStruct((B, S, DIFF_W), BF16),
        jax.ShapeDtypeStruct((B, S // ATT_TILE, DIFF_W, ATT_TILE), BF16),
        jax.ShapeDtypeStruct((B, S, DIFF_W), F32),
    )
    per_tile = ATT_TILE // tm
    out_specs = (row(N_RET_HEADS * RET_DK), row(N_RET_HEADS * RET_DK), row(RET_W), row(RET_W),
                 row(DIFF_W), row(DIFF_W),
                 pl.BlockSpec((None, None, DIFF_W, tm), lambda b, i: (b, i // per_tile, 0, i % per_tile)),
                 row(DIFF_W))
    return pl.pallas_call(
        _even_inproj_kernel,
        grid=(B, S // tm),
        in_specs=[row(D), _whole((1, D)), _whole(w.shape), _whole(wvt.shape), tab, tab],
        out_specs=out_specs,
        out_shape=out_shape,
        compiler_params=_params(("parallel", "parallel")),
        name="even_inproj",
    )(x, g, w, wvt, cos, sin)


def _retention_kernel(lg_ref, q_ref, k_ref, v_ref, g_ref, o_ref, lst_ref, st_ref, dmat_ref):
    n = pl.program_id(1)
    nblk = pl.num_programs(1) // 2
    C = RET_CHUNK
    cpb = RET_BLOCK // C
    heads = range(N_RET_HEADS)
    lf = [lg_ref[h] for h in heads]
    lb = [lg_ref[N_RET_HEADS + h] for h in heads]
    pos = lax.broadcasted_iota(jnp.int32, (C, 1), 0).astype(F32)

    def kcols(h):
        return slice(h * RET_DK, (h + 1) * RET_DK)

    def vcols(h):
        return slice(h * RET_DV, (h + 1) * RET_DV)

    def decay(x):
        return jnp.exp(jnp.full((1, RET_DV), x, F32))

    def kt_v(h, rows, w_col):
        kw = (k_ref[rows, kcols(h)].astype(F32) * w_col).T.astype(BF16)
        return jnp.dot(kw, v_ref[rows, vcols(h)], preferred_element_type=F32)

    @pl.when(n == 0)
    def _():
        r = lax.broadcasted_iota(jnp.int32, (C, C), 0)
        c = lax.broadcasted_iota(jnp.int32, (C, C), 1)
        d = (r - c).astype(F32)
        for h in heads:
            dmat_ref[h] = jnp.where(d >= 0, jnp.exp(lf[h] * jnp.maximum(d, 0.0)),
                                    jnp.exp(lb[h] * jnp.maximum(-d, 0.0)))

    @pl.when((n == 0) | (n == nblk))
    def _():
        st_ref[...] = jnp.zeros_like(st_ref)

    @pl.when(n < nblk)
    def _():
        blk = nblk - 1 - n
        for cc in reversed(range(cpb)):
            rows = slice(cc * C, (cc + 1) * C)
            chunk = blk * cpb + cc
            upd = [kt_v(h, rows, jnp.exp(lb[h] * pos)) for h in heads]
            for h in heads:
                lst_ref[chunk, h] = st_ref[h].astype(BF16)
                st_ref[h] = decay(lb[h] * C) * st_ref[h] + upd[h]

    @pl.when(n >= nblk)
    def _():
        blk = n - nblk
        for cc in range(cpb):
            rows = slice(cc * C, (cc + 1) * C)
            chunk = blk * cpb + cc
            q = [q_ref[rows, kcols(h)] for h in heads]
            s = [lax.dot_general(q[h], k_ref[rows, kcols(h)], (((1,), (1,)), ((), ())),
                                 preferred_element_type=F32) for h in heads]
            qf = [(q[h].astype(F32) * jnp.exp(lf[h] * (pos + 1.0))).astype(BF16) for h in heads]
            qb = [(q[h].astype(F32) * jnp.exp(lb[h] * (C - pos))).astype(BF16) for h in heads]
            sd = [(s[h] * dmat_ref[h]).astype(BF16) for h in heads]
            o = [jnp.dot(sd[h], v_ref[rows, vcols(h)], preferred_element_type=F32)
                 + jnp.dot(qf[h], st_ref[h].astype(BF16), preferred_element_type=F32)
                 + jnp.dot(qb[h], lst_ref[chunk, h], preferred_element_type=F32) for h in heads]
            upd = [kt_v(h, rows, jnp.exp(lf[h] * (C - 1.0 - pos))) for h in heads]
            for h in heads:
                st_ref[h] = decay(lf[h] * C) * st_ref[h] + upd[h]
                oh = o[h] * lax.rsqrt(jnp.mean(o[h] * o[h], axis=-1, keepdims=True) + EPS)
                o_ref[rows, vcols(h)] = (oh * _silu(g_ref[rows, vcols(h)])).astype(BF16)


def _retention(log_g, rq, rk, rv, rg):
    B, S, _ = rq.shape
    RB = RET_BLOCK
    NB = S // RB
    KW = N_RET_HEADS * RET_DK

    def both(n):
        return jnp.where(n < NB, NB - 1 - n, n - NB)

    def fwd(n):
        return jnp.maximum(n - NB, 0)

    q_spec = pl.BlockSpec((None, RB, KW), lambda b, n, lg: (b, fwd(n), 0))
    k_spec = pl.BlockSpec((None, RB, KW), lambda b, n, lg: (b, both(n), 0))
    v_spec = pl.BlockSpec((None, RB, RET_W), lambda b, n, lg: (b, both(n), 0))
    fwd_spec = pl.BlockSpec((None, RB, RET_W), lambda b, n, lg: (b, fwd(n), 0))
    grid_spec = pltpu.PrefetchScalarGridSpec(
        num_scalar_prefetch=1,
        grid=(B, 2 * NB),
        in_specs=[q_spec, k_spec, v_spec, fwd_spec],
        out_specs=fwd_spec,
        scratch_shapes=[pltpu.VMEM((S // RET_CHUNK, N_RET_HEADS, RET_DK, RET_DV), BF16),
                        pltpu.VMEM((N_RET_HEADS, RET_DK, RET_DV), F32),
                        pltpu.VMEM((N_RET_HEADS, RET_CHUNK, RET_CHUNK), F32)],
    )
    return pl.pallas_call(
        _retention_kernel,
        grid_spec=grid_spec,
        out_shape=jax.ShapeDtypeStruct((B, S, RET_W), BF16),
        compiler_params=_params(("parallel", "arbitrary")),
        name="retention",
    )(log_g, rq, rk, rv, rg)


def _diff_attn_kernel(sc_ref, q_ref, k_ref, vt_ref, btab_ref, g_ref, ng_ref, o_ref,
                      qz_ref, bias_ref, m_ref, acc_ref, s0_ref, p_ref, *, lam_init):
    h = pl.program_id(1)
    qi = pl.program_id(2)
    T = ATT_TILE
    nk = vt_ref.shape[0]
    lam = sc_ref[0]
    bias_left = sc_ref[1 + h]
    bias_right = sc_ref[1 + N_DIFF_HEADS + h]

    @pl.when(qi == 0)
    def _():
        for d in range(3):
            x = jnp.broadcast_to(btab_ref[d], (T, 2 * T))
            bias_ref[d] = pltpu.roll(x, 0, 1, stride=1, stride_axis=0)[:, :T]

    qt = q_ref[...].astype(F32).T
    half = lax.broadcasted_iota(jnp.int32, qt.shape, 0) < DIFF_DH
    qz_ref[0] = jnp.where(half, qt, 0.0).astype(BF16)
    qz_ref[1] = jnp.where(half, 0.0, qt).astype(BF16)
    m_ref[...] = jnp.full_like(m_ref, M_INIT)
    acc_ref[...] = jnp.zeros_like(acc_ref)
    ones = jnp.ones((ONES_ROWS, T), BF16)

    chains = [(ko, qb, c) for ko in range(ATT_QBLOCKS) for qb in range(ATT_QBLOCKS) for c in range(2)]

    def qcols(qb):
        return slice(qb * T, (qb + 1) * T)

    def scores(chain, kp):
        ko, qb, c = chain
        row0 = pl.multiple_of((kp * ATT_QBLOCKS + ko) * T, T)
        k = k_ref[pl.ds(row0, T), :]
        return jnp.dot(k, qz_ref[c, :, qcols(qb)], preferred_element_type=F32)

    def biased_rows(s, d):
        def rows(r):
            sr = s[r:r + ATT_ROWS]
            return sr if d is None else sr + bias_ref[d, r:r + ATT_ROWS, :]
        return rows

    def running_max(mx, x):
        return x if mx is None else jnp.maximum(mx, x)

    def softmax_pv(chain, vt, rows, mx, next_rows, p_ref):
        _, qb, c = chain
        cols = qcols(qb)
        m_old = m_ref[c, :, cols]
        m_new = jnp.maximum(m_old, jnp.max(mx, axis=0, keepdims=True))
        mx_next = None
        for r in range(0, T, ATT_ROWS):
            p_ref[r:r + ATT_ROWS, :] = jnp.exp2(rows(r) - m_new).astype(BF16)
            if next_rows is not None:
                mx_next = running_max(mx_next, next_rows(r))
        acc_ref[c, :, cols] = (jnp.exp2(m_old - m_new) * acc_ref[c, :, cols]
                               + jnp.dot(vt, p_ref[...], preferred_element_type=F32))
        m_ref[c, :, cols] = m_new
        return mx_next

    def step(kp, offset):
        vts = [jnp.concatenate([vt_ref[kp * ATT_QBLOCKS + ko], ones], axis=0)
               for ko in range(ATT_QBLOCKS)]
        kpn = jnp.minimum(kp + 1, nsteps - 1)

        def block_delta(chain):
            ko, qb, _ = chain
            return None if offset is None else offset * ATT_QBLOCKS + ko - qb

        def rows_of(i):
            delta = block_delta(chains[i])
            return biased_rows(s[i], delta + 1 if delta in (-1, 0, 1) else None)

        s = [s0_ref[i] for i in range(ATT_DEPTH)]
        rows = rows_of(0)
        mx = None
        for r in range(0, T, ATT_ROWS):
            mx = running_max(mx, rows(r))
        for i, chain in enumerate(chains):
            ko, qb, c = chain
            ahead = i + ATT_DEPTH
            if ahead < len(chains):
                s.append(scores(chains[ahead], kp))
            else:
                s0_ref[ahead - len(chains)] = scores(chains[ahead - len(chains)], kpn)
            delta = block_delta(chain)
            if c == 0 and delta == -1:
                m_ref[:, :, qcols(qb)] = m_ref[:, :, qcols(qb)] + bias_left
            if c == 0 and delta == 2:
                m_ref[:, :, qcols(qb)] = m_ref[:, :, qcols(qb)] - bias_right
            next_rows = rows_of(i + 1) if i + 1 < len(chains) else None
            mx = softmax_pv(chain, vts[ko], rows, mx, next_rows, p_ref.at[i % 2])
            rows = next_rows

    def far_step(kp, carry):
        step(kp, None)
        return carry

    nsteps = nk // ATT_QBLOCKS
    for i in range(ATT_DEPTH):
        s0_ref[i] = scores(chains[i], 0)
    lax.fori_loop(0, qi - 1, far_step, 0)
    for offset in (-1, 0, 1):
        kp = qi + offset

        @pl.when((kp >= 0) & (kp < nsteps))
        def _():
            step(jnp.clip(kp, 0, nsteps - 1), offset)

    lax.fori_loop(qi + 2, nsteps, far_step, 0)

    def normalised(c):
        return acc_ref[c, :DIFF_DV, :] / acc_ref[c, DIFF_DV:DIFF_DV + 1, :]

    ot = normalised(0) - lam * normalised(1)
    o = _rms(ot.T, ng_ref[...]) * (1.0 - lam_init)
    o_ref[...] = (o * _silu(g_ref[...])).astype(BF16)


def _diff_attn(scal, dq, dk, dvt, btab, dg, ng, lam_init):
    B, S, _ = dq.shape
    T = ATT_TILE
    TQ = ATT_QBLOCKS * T
    q_spec = pl.BlockSpec((None, TQ, DIFF_DV), lambda b, h, qi, sc: (b, qi, h))
    k_spec = pl.BlockSpec((None, S, DIFF_DV), lambda b, h, qi, sc: (b, 0, h))
    vt_spec = pl.BlockSpec((None, S // T, DIFF_DV, T), lambda b, h, qi, sc: (b, 0, h, 0))
    btab_spec = pl.BlockSpec((None, 3, 1, 2 * T), lambda b, h, qi, sc: (h, 0, 0, 0))
    ng_spec = pl.BlockSpec((1, DIFF_DV), lambda b, h, qi, sc: (0, 0))
    grid_spec = pltpu.PrefetchScalarGridSpec(
        num_scalar_prefetch=1,
        grid=(B, N_DIFF_HEADS, S // TQ),
        in_specs=[q_spec, k_spec, vt_spec, btab_spec, q_spec, ng_spec],
        out_specs=q_spec,
        scratch_shapes=[pltpu.VMEM((2, DIFF_DV, TQ), BF16),
                        pltpu.VMEM((3, T, T), F32),
                        pltpu.VMEM((2, 1, TQ), F32),
                        pltpu.VMEM((2, DIFF_DV + ONES_ROWS, TQ), F32),
                        pltpu.VMEM((ATT_DEPTH, T, T), F32),
                        pltpu.VMEM((2, T, T), BF16)],
    )
    return pl.pallas_call(
        functools.partial(_diff_attn_kernel, lam_init=lam_init),
        grid_spec=grid_spec,
        out_shape=jax.ShapeDtypeStruct((B, S, DIFF_W), BF16),
        compiler_params=_params(("parallel", "parallel", "arbitrary")),
        name="diff_attn",
    )(scal, dq, dk, dvt, btab, dg, ng)


def _mid_proj_kernel(x_ref, ro_ref, do_ref, wo_ref, g_ref, wi_ref, x1_ref, glu_ref, sg_ref):
    x1 = (x_ref[...]
          + jnp.dot(ro_ref[...], wo_ref[:RET_W, :], preferred_element_type=F32)
          + jnp.dot(do_ref[...], wo_ref[RET_W:, :], preferred_element_type=F32))
    x1_ref[...] = x1
    hb = _rms(x1, g_ref[...]).astype(BF16)
    step = 512
    for c0 in range(0, CONV_W, step):
        a = jnp.dot(hb, wi_ref[:, c0:c0 + step], preferred_element_type=F32)
        b = jnp.dot(hb, wi_ref[:, CONV_W + c0:CONV_W + c0 + step], preferred_element_type=F32)
        gate = jnp.dot(hb, wi_ref[:, 2 * CONV_W + c0:2 * CONV_W + c0 + step], preferred_element_type=F32)
        glu_ref[:, c0:c0 + step] = a * jax.nn.sigmoid(b)
        sg_ref[:, c0:c0 + step] = _silu(gate)


def _mid_proj(x, ro, do, wo, g, wi):
    B, S, D = x.shape
    tm = MID_TILE
    row = lambda n: pl.BlockSpec((None, tm, n), lambda b, i: (b, i, 0))
    return pl.pallas_call(
        _mid_proj_kernel,
        grid=(B, S // tm),
        in_specs=[row(D), row(RET_W), row(DIFF_W), _whole(wo.shape), _whole((1, D)), _whole(wi.shape)],
        out_specs=(row(D), row(CONV_W), row(CONV_W)),
        out_shape=(jax.ShapeDtypeStruct((B, S, D), F32),
                   jax.ShapeDtypeStruct((B, S, CONV_W), F32),
                   jax.ShapeDtypeStruct((B, S, CONV_W), F32)),
        compiler_params=_params(("parallel", "parallel")),
        name="mid_proj",
    )(x, ro, do, wo, g, wi)


def _conv_out_kernel(glu_ref, prev_ref, next_ref, sg_ref, x1_ref, cw_ref, cb_ref, ng_ref, nb_ref,
                     wo_ref, fg_ref, o_ref, win_ref, y_ref):
    i = pl.program_id(1)
    nblk = pl.num_programs(1)
    TS = CONV_TILE
    ncb = CONV_W // LANES
    prev = jnp.where(i > 0, prev_ref[...], 0.0)
    nxt = jnp.where(i < nblk - 1, next_ref[...], 0.0)
    for cb in range(ncb):
        sl = slice(cb * LANES, (cb + 1) * LANES)
        win_ref[cb, 0:HALO, :] = prev[:, sl]
        win_ref[cb, HALO:HALO + TS, :] = glu_ref[:, sl]
        win_ref[cb, HALO + TS:HALO + TS + HALO, :] = nxt[:, sl]

    def chan_block(cb, carry):
        w = cw_ref[cb]
        for r0 in range(0, TS, CONV_ROWS):
            acc = jnp.zeros((CONV_ROWS, LANES), F32)
            for t in range(CONV_K):
                off = r0 + HALO - CONV_PAD + t
                acc = acc + win_ref[cb, pl.ds(off, CONV_ROWS), :] * w[t:t + 1, :]
            y_ref[cb, r0:r0 + CONV_ROWS, :] = acc
        return carry

    lax.fori_loop(0, ncb, chan_block, 0)

    y = jnp.concatenate([y_ref[cb] for cb in range(ncb)], axis=-1) + cb_ref[...]
    mu = jnp.mean(y, axis=-1, keepdims=True)
    yc = y - mu
    var = jnp.mean(yc * yc, axis=-1, keepdims=True)
    yn = yc * lax.rsqrt(var + EPS) * ng_ref[...] + nb_ref[...]
    z = (_silu(yn) * sg_ref[...]).astype(BF16)
    out = x1_ref[...] + jnp.dot(z, wo_ref[...], preferred_element_type=F32)
    o_ref[...] = _rms(out, fg_ref[...])


def _conv_out(glu, sg, x1, cw, cb, ng, nb, wo, fg):
    B, S, D = x1.shape
    TS = CONV_TILE
    hb = TS // HALO
    nh = S // HALO
    row = lambda n: pl.BlockSpec((None, TS, n), lambda b, i: (b, i, 0))
    prev_spec = pl.BlockSpec((None, HALO, CONV_W), lambda b, i: (b, jnp.maximum(i * hb - 1, 0), 0))
    next_spec = pl.BlockSpec((None, HALO, CONV_W), lambda b, i: (b, jnp.minimum((i + 1) * hb, nh - 1), 0))
    ncb = CONV_W // LANES
    return pl.pallas_call(
        _conv_out_kernel,
        grid=(B, S // TS),
        in_specs=[row(CONV_W), prev_spec, next_spec, row(CONV_W), row(D),
                  _whole(cw.shape), _whole((1, CONV_W)), _whole((1, CONV_W)), _whole((1, CONV_W)),
                  _whole(wo.shape), _whole((1, D))],
        out_specs=row(D),
        out_shape=jax.ShapeDtypeStruct((B, S, D), F32),
        scratch_shapes=[pltpu.VMEM((ncb, TS + 2 * HALO, LANES), F32),
                        pltpu.VMEM((ncb, TS, LANES), F32)],
        compiler_params=_params(("parallel", "parallel")),
        name="conv_out",
    )(glu, glu, glu, sg, x1, cw, cb, ng, nb, wo, fg)


def _t5_bucket_np(rel):
    nb = T5_BUCKETS // 2
    max_exact = nb // 2
    ret = (rel > 0).astype(np.int64) * nb
    n = np.abs(rel)
    nf = np.maximum(n, 1).astype(np.float64)
    large = max_exact + (np.log(nf / max_exact) / math.log(T5_MAX_DIST / max_exact)
                         * (nb - max_exact)).astype(np.int64)
    large = np.minimum(large, nb - 1)
    return ret + np.where(n < max_exact, n, large)


def _rope_tables(S):
    half = RET_DK // 2
    inv = 1.0 / (ROPE_BASE ** (jnp.arange(0, RET_DK, 2, dtype=F32) / RET_DK))
    ang = jnp.arange(S).astype(F32)[:, None] * inv[None, :]
    cos = jnp.cos(ang)
    sin = jnp.sin(ang)
    assert cos.shape == (S, half)
    return jnp.concatenate([cos, cos], axis=-1), jnp.concatenate([-sin, sin], axis=-1)


def _trunk(x, p):
    B, S, D = x.shape
    assert D == D_MODEL and ATT_TILE % ROW_TILE == 0
    assert all(S % t == 0 for t in (ATT_TILE * ATT_QBLOCKS, RET_BLOCK, ROW_TILE, MID_TILE, CONV_TILE))

    cos, sin = _rope_tables(S)
    rq, rk, rv, rg, dq, dk, dvt, dg = _even_inproj(x, p["g0"], p["w_in_mix"], p["wvt"], cos, sin)
    ro = _retention(p["log_g"], rq, rk, rv, rg)
    do = _diff_attn(p["attn_scal"], dq, dk, dvt, p["btab"], dg, p["diff_norm_g"], p["lam_init"])
    x1, glu, sg = _mid_proj(x, ro, do, p["w_out_mix"], p["g1"], p["w_in_conv"])
    return _conv_out(glu, sg, x1, p["conv_w"], p["conv_b"], p["conv_norm_g"], p["conv_norm_b"],
                     p["w_out_conv"], p["final_g"])


def kernel(x_prompt, x_sample, norm_g, final_g, rel_bias, w_in_mix, ret_decay, lam_q1, lam_k1, lam_q2,
           lam_k2, diff_norm_g, w_out_mix, w_in_conv, conv_w, conv_b, conv_norm_g, conv_norm_b,
           w_out_conv):
    assert norm_g.shape[0] == 2 and w_in_mix.shape[0] == 1 and w_in_conv.shape[0] == 1
    lam_init = 0.8 - 0.6 * math.exp(-0.3 * 0)
    lam = (jnp.exp(jnp.sum(lam_q1[0].astype(F32) * lam_k1[0].astype(F32)))
           - jnp.exp(jnp.sum(lam_q2[0].astype(F32) * lam_k2[0].astype(F32))) + lam_init)
    T = ATT_TILE
    i = np.arange(2 * T)
    idx = np.stack([_t5_bucket_np(np.where(i <= T, off * T - i, off * T + 2 * T - i))
                    for off in (-1, 0, 1)]).astype(np.int32)
    bias2 = rel_bias.astype(F32) * LOG2E
    btab = jnp.take(bias2.T, jnp.asarray(idx), axis=1).reshape(N_DIFF_HEADS, 3, 1, 2 * T)
    far_left = int(_t5_bucket_np(np.array(-T5_MAX_DIST)))
    far_right = int(_t5_bucket_np(np.array(T5_MAX_DIST)))
    attn_scal = jnp.concatenate([lam.reshape(1), bias2[far_left], bias2[far_right]])
    wim = w_in_mix[0]
    ncb = CONV_W // LANES
    p = dict(
        g0=norm_g[0].reshape(1, D_MODEL), g1=norm_g[1].reshape(1, D_MODEL),
        final_g=final_g.reshape(1, D_MODEL),
        w_in_mix=wim.astype(BF16),
        wvt=wim[:, _C_DV:_C_DV + DIFF_W].T.astype(BF16),
        log_g=(-jnp.exp(ret_decay[0].astype(F32))).reshape(-1),
        attn_scal=attn_scal, btab=btab, lam_init=lam_init,
        diff_norm_g=diff_norm_g[0].reshape(1, DIFF_DV),
        w_out_mix=w_out_mix[0].astype(BF16),
        w_in_conv=w_in_conv[0].astype(BF16),
        conv_w=conv_w[0].reshape(CONV_K, ncb, LANES).transpose(1, 0, 2),
        conv_b=conv_b[0].reshape(1, CONV_W),
        conv_norm_g=conv_norm_g[0].reshape(1, CONV_W),
        conv_norm_b=conv_norm_b[0].reshape(1, CONV_W),
        w_out_conv=w_out_conv[0].astype(BF16),
    )
    return (_trunk(x_prompt, p), _trunk(x_sample, p))
```

```python
import functools
import math

import numpy as np
import jax
import jax.numpy as jnp
from jax import lax
from jax.experimental import pallas as pl
from jax.experimental.pallas import tpu as pltpu

F32 = jnp.float32
BF16 = jnp.bfloat16

D_MODEL = 1024
N_RET_HEADS = 4
RET_DK = 128
RET_DV = 256
N_DIFF_HEADS = 8
DIFF_DH = 64
DIFF_DV = 2 * DIFF_DH
RET_W = N_RET_HEADS * RET_DV
DIFF_W = N_DIFF_HEADS * DIFF_DV
CONV_W = 2 * D_MODEL
CONV_K = 31
CONV_PAD = CONV_K // 2
T5_BUCKETS = 32
T5_MAX_DIST = 128
ROPE_BASE = 10000.0
EPS = 1e-6

_C_RQ = 0
_C_RK = _C_RQ + N_RET_HEADS * RET_DK
_C_RV = _C_RK + N_RET_HEADS * RET_DK
_C_RG = _C_RV + RET_W
_C_DQ = _C_RG + RET_W
_C_DK = _C_DQ + DIFF_W
_C_DV = _C_DK + DIFF_W
_C_DG = _C_DV + DIFF_W
EVEN_IN_W = _C_DG + DIFF_W

LANES = 128
HALO = 16
VMEM_LIMIT = 56 * 1024 * 1024

ROW_TILE = 256
MID_TILE = 256
RET_CHUNK = 256
RET_BLOCK = 1024
ATT_TILE = 256
ATT_QBLOCKS = 4
ATT_ROWS = 32
ATT_DEPTH = 4
ONES_ROWS = 16
LOG2E = math.log2(math.e)
CONV_TILE = 512
CONV_ROWS = 64

M_INIT = -1e30


def _rms(x, g):
    return x * lax.rsqrt(jnp.mean(x * x, axis=-1, keepdims=True) + EPS) * g


def _silu(x):
    return x * jax.nn.sigmoid(x)


def _params(sem):
    return pltpu.CompilerParams(dimension_semantics=sem, vmem_limit_bytes=VMEM_LIMIT)


def _whole(shape):
    nd = len(shape)
    return pl.BlockSpec(shape, lambda *_: (0,) * nd, pipeline_mode=pl.Buffered(1))


def _even_inproj_kernel(x_ref, g_ref, w_ref, cos_ref, sin_ref,
                        rq_ref, rk_ref, rv_ref, rg_ref, dq_ref, dk_ref, dvt_ref, dg_ref):
    hb = _rms(x_ref[...], g_ref[...]).astype(BF16)

    def proj(c0, n):
        return jnp.dot(hb, w_ref[:, c0:c0 + n], preferred_element_type=F32)

    cos = cos_ref[...]
    sin = sin_ref[...]

    def rope(u):
        return u * cos + pltpu.roll(u, RET_DK // 2, 1) * sin

    uq = proj(_C_RQ, N_RET_HEADS * RET_DK)
    uk = proj(_C_RK, N_RET_HEADS * RET_DK)
    for j in range(N_RET_HEADS):
        sl = slice(j * RET_DK, (j + 1) * RET_DK)
        rq_ref[:, sl] = rope(uq[:, sl]).astype(BF16)
        rk_ref[:, sl] = (rope(uk[:, sl]) * (RET_DK ** -0.5)).astype(BF16)
    rv_ref[...] = proj(_C_RV, RET_W).astype(BF16)
    rg_ref[...] = proj(_C_RG, RET_W)
    dq_ref[...] = (proj(_C_DQ, DIFF_W) * (DIFF_DH ** -0.5 * LOG2E)).astype(BF16)
    dk_ref[...] = proj(_C_DK, DIFF_W).astype(BF16)
    dg_ref[...] = proj(_C_DG, DIFF_W)
    dvt_ref[...] = proj(_C_DV, DIFF_W).T.astype(BF16)


def _even_inproj(x, g, w, cos, sin):
    B, S, D = x.shape
    tm = ROW_TILE
    row = lambda n: pl.BlockSpec((None, tm, n), lambda b, i: (b, i, 0))
    tab = pl.BlockSpec((tm, LANES), lambda b, i: (i, 0))
    out_shape = (
        jax.ShapeDtypeStruct((B, S, N_RET_HEADS * RET_DK), BF16),
        jax.ShapeDtypeStruct((B, S, N_RET_HEADS * RET_DK), BF16),
        jax.ShapeDtypeStruct((B, S, RET_W), BF16),
        jax.ShapeDtypeStruct((B, S, RET_W), F32),
        jax.ShapeDtypeStruct((B, S, DIFF_W), BF16),
        jax.ShapeDtypeStruct((B, S, DIFF_W), BF16),
        jax.ShapeDtypeStruct((B, S // ATT_TILE, DIFF_W, ATT_TILE), BF16),
        jax.ShapeDtypeStruct((B, S, DIFF_W), F32),
    )
    per_tile = ATT_TILE // tm
    out_specs = (row(N_RET_HEADS * RET_DK), row(N_RET_HEADS * RET_DK), row(RET_W), row(RET_W),
                 row(DIFF_W), row(DIFF_W),
                 pl.BlockSpec((None, None, DIFF_W, tm), lambda b, i: (b, i // per_tile, 0, i % per_tile)),
                 row(DIFF_W))
    return pl.pallas_call(
        _even_inproj_kernel,
        grid=(B, S // tm),
        in_specs=[row(D), _whole((1, D)), _whole(w.shape), tab, tab],
        out_specs=out_specs,
        out_shape=out_shape,
        compiler_params=_params(("parallel", "parallel")),
        name="even_inproj",
    )(x, g, w, cos, sin)


def _retention_kernel(lg_ref, q_ref, k_ref, v_ref, g_ref, o_ref, lst_ref, st_ref, dmat_ref):
    n = pl.program_id(1)
    nblk = pl.num_programs(1) // 2
    C = RET_CHUNK
    cpb = RET_BLOCK // C
    heads = range(N_RET_HEADS)
    lf = [lg_ref[h] for h in heads]
    lb = [lg_ref[N_RET_HEADS + h] for h in heads]
    pos = lax.broadcasted_iota(jnp.int32, (C, 1), 0).astype(F32)

    def kcols(h):
        return slice(h * RET_DK, (h + 1) * RET_DK)

    def vcols(h):
        return slice(h * RET_DV, (h + 1) * RET_DV)

    def decay(x):
        return jnp.exp(jnp.full((1, RET_DV), x, F32))

    def kt_v(h, rows, w_col):
        kw = (k_ref[rows, kcols(h)].astype(F32) * w_col).T.astype(BF16)
        return jnp.dot(kw, v_ref[rows, vcols(h)], preferred_element_type=F32)

    @pl.when(n == 0)
    def _():
        r = lax.broadcasted_iota(jnp.int32, (C, C), 0)
        c = lax.broadcasted_iota(jnp.int32, (C, C), 1)
        d = (r - c).astype(F32)
        for h in heads:
            dmat_ref[h] = jnp.where(d >= 0, jnp.exp(lf[h] * jnp.maximum(d, 0.0)),
                                    jnp.exp(lb[h] * jnp.maximum(-d, 0.0)))

    @pl.when((n == 0) | (n == nblk))
    def _():
        st_ref[...] = jnp.zeros_like(st_ref)

    @pl.when(n < nblk)
    def _():
        blk = nblk - 1 - n
        for cc in reversed(range(cpb)):
            rows = slice(cc * C, (cc + 1) * C)
            chunk = blk * cpb + cc
            upd = [kt_v(h, rows, jnp.exp(lb[h] * pos)) for h in heads]
            for h in heads:
                lst_ref[chunk, h] = st_ref[h].astype(BF16)
                st_ref[h] = decay(lb[h] * C) * st_ref[h] + upd[h]

    @pl.when(n >= nblk)
    def _():
        blk = n - nblk
        for cc in range(cpb):
            rows = slice(cc * C, (cc + 1) * C)
            chunk = blk * cpb + cc
            q = [q_ref[rows, kcols(h)] for h in heads]
            s = [lax.dot_general(q[h], k_ref[rows, kcols(h)], (((1,), (1,)), ((), ())),
                                 preferred_element_type=F32) for h in heads]
            qf = [(q[h].astype(F32) * jnp.exp(lf[h] * (pos + 1.0))).astype(BF16) for h in heads]
            qb = [(q[h].astype(F32) * jnp.exp(lb[h] * (C - pos))).astype(BF16) for h in heads]
            sd = [(s[h] * dmat_ref[h]).astype(BF16) for h in heads]
            o = [jnp.dot(sd[h], v_ref[rows, vcols(h)], preferred_element_type=F32)
                 + jnp.dot(qf[h], st_ref[h].astype(BF16), preferred_element_type=F32)
                 + jnp.dot(qb[h], lst_ref[chunk, h], preferred_element_type=F32) for h in heads]
            upd = [kt_v(h, rows, jnp.exp(lf[h] * (C - 1.0 - pos))) for h in heads]
            for h in heads:
                st_ref[h] = decay(lf[h] * C) * st_ref[h] + upd[h]
                oh = o[h] * lax.rsqrt(jnp.mean(o[h] * o[h], axis=-1, keepdims=True) + EPS)
                o_ref[rows, vcols(h)] = (oh * _silu(g_ref[rows, vcols(h)])).astype(BF16)


def _retention(log_g, rq, rk, rv, rg):
    B, S, _ = rq.shape
    RB = RET_BLOCK
    NB = S // RB
    KW = N_RET_HEADS * RET_DK

    def both(n):
        return jnp.where(n < NB, NB - 1 - n, n - NB)

    def fwd(n):
        return jnp.maximum(n - NB, 0)

    q_spec = pl.BlockSpec((None, RB, KW), lambda b, n, lg: (b, fwd(n), 0))
    k_spec = pl.BlockSpec((None, RB, KW), lambda b, n, lg: (b, both(n), 0))
    v_spec = pl.BlockSpec((None, RB, RET_W), lambda b, n, lg: (b, both(n), 0))
    fwd_spec = pl.BlockSpec((None, RB, RET_W), lambda b, n, lg: (b, fwd(n), 0))
    grid_spec = pltpu.PrefetchScalarGridSpec(
        num_scalar_prefetch=1,
        grid=(B, 2 * NB),
        in_specs=[q_spec, k_spec, v_spec, fwd_spec],
        out_specs=fwd_spec,
        scratch_shapes=[pltpu.VMEM((S // RET_CHUNK, N_RET_HEADS, RET_DK, RET_DV), BF16),
                        pltpu.VMEM((N_RET_HEADS, RET_DK, RET_DV), F32),
                        pltpu.VMEM((N_RET_HEADS, RET_CHUNK, RET_CHUNK), F32)],
    )
    return pl.pallas_call(
        _retention_kernel,
        grid_spec=grid_spec,
        out_shape=jax.ShapeDtypeStruct((B, S, RET_W), BF16),
        compiler_params=_params(("parallel", "arbitrary")),
        name="retention",
    )(log_g, rq, rk, rv, rg)


def _diff_attn_kernel(sc_ref, q_ref, k_ref, vt_ref, btab_ref, g_ref, ng_ref, o_ref,
                      qz_ref, bias_ref, m_ref, acc_ref, s0_ref, p_ref, *, lam_init):
    h = pl.program_id(1)
    qi = pl.program_id(2)
    T = ATT_TILE
    nk = vt_ref.shape[0]
    lam = sc_ref[0]
    bias_left = sc_ref[1 + h]
    bias_right = sc_ref[1 + N_DIFF_HEADS + h]

    @pl.when(qi == 0)
    def _():
        for d in range(3):
            x = jnp.broadcast_to(btab_ref[d], (T, 2 * T))
            bias_ref[d] = pltpu.roll(x, 0, 1, stride=1, stride_axis=0)[:, :T]

    qt = q_ref[...].astype(F32).T
    half = lax.broadcasted_iota(jnp.int32, qt.shape, 0) < DIFF_DH
    qz_ref[0] = jnp.where(half, qt, 0.0).astype(BF16)
    qz_ref[1] = jnp.where(half, 0.0, qt).astype(BF16)
    m_ref[...] = jnp.full_like(m_ref, M_INIT)
    acc_ref[...] = jnp.zeros_like(acc_ref)
    ones = jnp.ones((ONES_ROWS, T), BF16)

    chains = [(ko, qb, c) for ko in range(ATT_QBLOCKS) for qb in range(ATT_QBLOCKS) for c in range(2)]

    def qcols(qb):
        return slice(qb * T, (qb + 1) * T)

    def scores(chain, kp):
        ko, qb, c = chain
        row0 = pl.multiple_of((kp * ATT_QBLOCKS + ko) * T, T)
        k = k_ref[pl.ds(row0, T), :]
        return jnp.dot(k, qz_ref[c, :, qcols(qb)], preferred_element_type=F32)

    def biased_rows(s, d):
        def rows(r):
            sr = s[r:r + ATT_ROWS]
            return sr if d is None else sr + bias_ref[d, r:r + ATT_ROWS, :]
        return rows

    def running_max(mx, x):
        return x if mx is None else jnp.maximum(mx, x)

    def softmax_pv(chain, vt, rows, mx, next_rows, p_ref):
        _, qb, c = chain
        cols = qcols(qb)
        m_old = m_ref[c, :, cols]
        m_new = jnp.maximum(m_old, jnp.max(mx, axis=0, keepdims=True))
        mx_next = None
        for r in range(0, T, ATT_ROWS):
            p_ref[r:r + ATT_ROWS, :] = jnp.exp2(rows(r) - m_new).astype(BF16)
            if next_rows is not None:
                mx_next = running_max(mx_next, next_rows(r))
        acc_ref[c, :, cols] = (jnp.exp2(m_old - m_new) * acc_ref[c, :, cols]
                               + jnp.dot(vt, p_ref[...], preferred_element_type=F32))
        m_ref[c, :, cols] = m_new
        return mx_next

    def step(kp, offset):
        vts = [jnp.concatenate([vt_ref[kp * ATT_QBLOCKS + ko], ones], axis=0)
               for ko in range(ATT_QBLOCKS)]
        kpn = jnp.minimum(kp + 1, nsteps - 1)

        def block_delta(chain):
            ko, qb, _ = chain
            return None if offset is None else offset * ATT_QBLOCKS + ko - qb

        def rows_of(i):
            delta = block_delta(chains[i])
            return biased_rows(s[i], delta + 1 if delta in (-1, 0, 1) else None)

        s = [s0_ref[i] for i in range(ATT_DEPTH)]
        rows = rows_of(0)
        mx = None
        for r in range(0, T, ATT_ROWS):
            mx = running_max(mx, rows(r))
        for i, chain in enumerate(chains):
            ko, qb, c = chain
            ahead = i + ATT_DEPTH
            if ahead < len(chains):
                s.append(scores(chains[ahead], kp))
            else:
                s0_ref[ahead - len(chains)] = scores(chains[ahead - len(chains)], kpn)
            delta = block_delta(chain)
            if c == 0 and delta == -1:
                m_ref[:, :, qcols(qb)] = m_ref[:, :, qcols(qb)] + bias_left
            if c == 0 and delta == 2:
                m_ref[:, :, qcols(qb)] = m_ref[:, :, qcols(qb)] - bias_right
            next_rows = rows_of(i + 1) if i + 1 < len(chains) else None
            mx = softmax_pv(chain, vts[ko], rows, mx, next_rows, p_ref.at[i % 2])
            rows = next_rows

    def far_step(kp, carry):
        step(kp, None)
        return carry

    nsteps = nk // ATT_QBLOCKS
    for i in range(ATT_DEPTH):
        s0_ref[i] = scores(chains[i], 0)
    lax.fori_loop(0, qi - 1, far_step, 0)
    for offset in (-1, 0, 1):
        kp = qi + offset

        @pl.when((kp >= 0) & (kp < nsteps))
        def _():
            step(jnp.clip(kp, 0, nsteps - 1), offset)

    lax.fori_loop(qi + 2, nsteps, far_step, 0)

    def normalised(c):
        return acc_ref[c, :DIFF_DV, :] / acc_ref[c, DIFF_DV:DIFF_DV + 1, :]

    ot = normalised(0) - lam * normalised(1)
    o = _rms(ot.T, ng_ref[...]) * (1.0 - lam_init)
    o_ref[...] = (o * _silu(g_ref[...])).astype(BF16)


def _diff_attn(scal, dq, dk, dvt, btab, dg, ng, lam_init):
    B, S, _ = dq.shape
    T = ATT_TILE
    TQ = ATT_QBLOCKS * T
    q_spec = pl.BlockSpec((None, TQ, DIFF_DV), lambda b, h, qi, sc: (b, qi, h))
    k_spec = pl.BlockSpec((None, S, DIFF_DV), lambda b, h, qi, sc: (b, 0, h))
    vt_spec = pl.BlockSpec((None, S // T, DIFF_DV, T), lambda b, h, qi, sc: (b, 0, h, 0))
    btab_spec = pl.BlockSpec((None, 3, 1, 2 * T), lambda b, h, qi, sc: (h, 0, 0, 0))
    ng_spec = pl.BlockSpec((1, DIFF_DV), lambda b, h, qi, sc: (0, 0))
    grid_spec = pltpu.PrefetchScalarGridSpec(
        num_scalar_prefetch=1,
        grid=(B, N_DIFF_HEADS, S // TQ),
        in_specs=[q_spec, k_spec, vt_spec, btab_spec, q_spec, ng_spec],
        out_specs=q_spec,
        scratch_shapes=[pltpu.VMEM((2, DIFF_DV, TQ), BF16),
                        pltpu.VMEM((3, T, T), F32),
                        pltpu.VMEM((2, 1, TQ), F32),
                        pltpu.VMEM((2, DIFF_DV + ONES_ROWS, TQ), F32),
                        pltpu.VMEM((ATT_DEPTH, T, T), F32),
                        pltpu.VMEM((2, T, T), BF16)],
    )
    return pl.pallas_call(
        functools.partial(_diff_attn_kernel, lam_init=lam_init),
        grid_spec=grid_spec,
        out_shape=jax.ShapeDtypeStruct((B, S, DIFF_W), BF16),
        compiler_params=_params(("parallel", "parallel", "arbitrary")),
        name="diff_attn",
    )(scal, dq, dk, dvt, btab, dg, ng)


def _mid_proj_kernel(x_ref, ro_ref, do_ref, wo_ref, g_ref, wi_ref, x1_ref, glu_ref, sg_ref):
    x1 = (x_ref[...]
          + jnp.dot(ro_ref[...], wo_ref[:RET_W, :], preferred_element_type=F32)
          + jnp.dot(do_ref[...], wo_ref[RET_W:, :], preferred_element_type=F32))
    x1_ref[...] = x1
    hb = _rms(x1, g_ref[...]).astype(BF16)
    step = 512
    for c0 in range(0, CONV_W, step):
        a = jnp.dot(hb, wi_ref[:, c0:c0 + step], preferred_element_type=F32)
        b = jnp.dot(hb, wi_ref[:, CONV_W + c0:CONV_W + c0 + step], preferred_element_type=F32)
        gate = jnp.dot(hb, wi_ref[:, 2 * CONV_W + c0:2 * CONV_W + c0 + step], preferred_element_type=F32)
        glu_ref[:, c0:c0 + step] = a * jax.nn.sigmoid(b)
        sg_ref[:, c0:c0 + step] = _silu(gate)


def _mid_proj(x, ro, do, wo, g, wi):
    B, S, D = x.shape
    tm = MID_TILE
    row = lambda n: pl.BlockSpec((None, tm, n), lambda b, i: (b, i, 0))
    return pl.pallas_call(
        _mid_proj_kernel,
        grid=(B, S // tm),
        in_specs=[row(D), row(RET_W), row(DIFF_W), _whole(wo.shape), _whole((1, D)), _whole(wi.shape)],
        out_specs=(row(D), row(CONV_W), row(CONV_W)),
        out_shape=(jax.ShapeDtypeStruct((B, S, D), F32),
                   jax.ShapeDtypeStruct((B, S, CONV_W), F32),
                   jax.ShapeDtypeStruct((B, S, CONV_W), F32)),
        compiler_params=_params(("parallel", "parallel")),
        name="mid_proj",
    )(x, ro, do, wo, g, wi)


def _conv_out_kernel(glu_ref, prev_ref, next_ref, sg_ref, x1_ref, cw_ref, cb_ref, ng_ref, nb_ref,
                     wo_ref, fg_ref, o_ref, win_ref, y_ref):
    i = pl.program_id(1)
    nblk = pl.num_programs(1)
    TS = CONV_TILE
    ncb = CONV_W // LANES
    prev = jnp.where(i > 0, prev_ref[...], 0.0)
    nxt = jnp.where(i < nblk - 1, next_ref[...], 0.0)
    for cb in range(ncb):
        sl = slice(cb * LANES, (cb + 1) * LANES)
        win_ref[cb, 0:HALO, :] = prev[:, sl]
        win_ref[cb, HALO:HALO + TS, :] = glu_ref[:, sl]
        win_ref[cb, HALO + TS:HALO + TS + HALO, :] = nxt[:, sl]

    def chan_block(cb, carry):
        w = cw_ref[cb]
        for r0 in range(0, TS, CONV_ROWS):
            acc = jnp.zeros((CONV_ROWS, LANES), F32)
            for t in range(CONV_K):
                off = r0 + HALO - CONV_PAD + t
                acc = acc + win_ref[cb, pl.ds(off, CONV_ROWS), :] * w[t:t + 1, :]
            y_ref[cb, r0:r0 + CONV_ROWS, :] = acc
        return carry

    lax.fori_loop(0, ncb, chan_block, 0)

    y = jnp.concatenate([y_ref[cb] for cb in range(ncb)], axis=-1) + cb_ref[...]
    mu = jnp.mean(y, axis=-1, keepdims=True)
    yc = y - mu
    var = jnp.mean(yc * yc, axis=-1, keepdims=True)
    yn = yc * lax.rsqrt(var + EPS) * ng_ref[...] + nb_ref[...]
    z = (_silu(yn) * sg_ref[...]).astype(BF16)
    out = x1_ref[...] + jnp.dot(z, wo_ref[...], preferred_element_type=F32)
    o_ref[...] = _rms(out, fg_ref[...])


def _conv_out(glu, sg, x1, cw, cb, ng, nb, wo, fg):
    B, S, D = x1.shape
    TS = CONV_TILE
    hb = TS // HALO
    nh = S // HALO
    row = lambda n: pl.BlockSpec((None, TS, n), lambda b, i: (b, i, 0))
    prev_spec = pl.BlockSpec((None, HALO, CONV_W), lambda b, i: (b, jnp.maximum(i * hb - 1, 0), 0))
    next_spec = pl.BlockSpec((None, HALO, CONV_W), lambda b, i: (b, jnp.minimum((i + 1) * hb, nh - 1), 0))
    ncb = CONV_W // LANES
    return pl.pallas_call(
        _conv_out_kernel,
        grid=(B, S // TS),
        in_specs=[row(CONV_W), prev_spec, next_spec, row(CONV_W), row(D),
                  _whole(cw.shape), _whole((1, CONV_W)), _whole((1, CONV_W)), _whole((1, CONV_W)),
                  _whole(wo.shape), _whole((1, D))],
        out_specs=row(D),
        out_shape=jax.ShapeDtypeStruct((B, S, D), F32),
        scratch_shapes=[pltpu.VMEM((ncb, TS + 2 * HALO, LANES), F32),
                        pltpu.VMEM((ncb, TS, LANES), F32)],
        compiler_params=_params(("parallel", "parallel")),
        name="conv_out",
    )(glu, glu, glu, sg, x1, cw, cb, ng, nb, wo, fg)


def _t5_bucket_np(rel):
    nb = T5_BUCKETS // 2
    max_exact = nb // 2
    ret = (rel > 0).astype(np.int64) * nb
    n = np.abs(rel)
    nf = np.maximum(n, 1).astype(np.float64)
    large = max_exact + (np.log(nf / max_exact) / math.log(T5_MAX_DIST / max_exact)
                         * (nb - max_exact)).astype(np.int64)
    large = np.minimum(large, nb - 1)
    return ret + np.where(n < max_exact, n, large)


def _rope_tables(S):
    half = RET_DK // 2
    inv = 1.0 / (ROPE_BASE ** (jnp.arange(0, RET_DK, 2, dtype=F32) / RET_DK))
    ang = jnp.arange(S).astype(F32)[:, None] * inv[None, :]
    cos = jnp.cos(ang)
    sin = jnp.sin(ang)
    assert cos.shape == (S, half)
    return jnp.concatenate([cos, cos], axis=-1), jnp.concatenate([-sin, sin], axis=-1)


def _trunk(x, p):
    B, S, D = x.shape
    assert D == D_MODEL and ATT_TILE % ROW_TILE == 0
    assert all(S % t == 0 for t in (ATT_TILE * ATT_QBLOCKS, RET_BLOCK, ROW_TILE, MID_TILE, CONV_TILE))

    cos, sin = _rope_tables(S)
    rq, rk, rv, rg, dq, dk, dvt, dg = _even_inproj(x, p["g0"], p["w_in_mix"], cos, sin)
    ro = _retention(p["log_g"], rq, rk, rv, rg)
    do = _diff_attn(p["attn_scal"], dq, dk, dvt, p["btab"], dg, p["diff_norm_g"], p["lam_init"])
    x1, glu, sg = _mid_proj(x, ro, do, p["w_out_mix"], p["g1"], p["w_in_conv"])
    return _conv_out(glu, sg, x1, p["conv_w"], p["conv_b"], p["conv_norm_g"], p["conv_norm_b"],
                     p["w_out_conv"], p["final_g"])


def kernel(x_prompt, x_sample, norm_g, final_g, rel_bias, w_in_mix, ret_decay, lam_q1, lam_k1, lam_q2,
           lam_k2, diff_norm_g, w_out_mix, w_in_conv, conv_w, conv_b, conv_norm_g, conv_norm_b,
           w_out_conv):
    assert norm_g.shape[0] == 2 and w_in_mix.shape[0] == 1 and w_in_conv.shape[0] == 1
    lam_init = 0.8 - 0.6 * math.exp(-0.3 * 0)
    lam = (jnp.exp(jnp.sum(lam_q1[0].astype(F32) * lam_k1[0].astype(F32)))
           - jnp.exp(jnp.sum(lam_q2[0].astype(F32) * lam_k2[0].astype(F32))) + lam_init)
    T = ATT_TILE
    i = np.arange(2 * T)
    idx = np.stack([_t5_bucket_np(np.where(i <= T, off * T - i, off * T + 2 * T - i))
                    for off in (-1, 0, 1)]).astype(np.int32)
    bias2 = rel_bias.astype(F32) * LOG2E
    btab = jnp.take(bias2.T, jnp.asarray(idx), axis=1).reshape(N_DIFF_HEADS, 3, 1, 2 * T)
    far_left = int(_t5_bucket_np(np.array(-T5_MAX_DIST)))
    far_right = int(_t5_bucket_np(np.array(T5_MAX_DIST)))
    attn_scal = jnp.concatenate([lam.reshape(1), bias2[far_left], bias2[far_right]])
    wim = w_in_mix[0]
    ncb = CONV_W // LANES
    p = dict(
        g0=norm_g[0].reshape(1, D_MODEL), g1=norm_g[1].reshape(1, D_MODEL),
        final_g=final_g.reshape(1, D_MODEL),
        w_in_mix=wim.astype(BF16),
        log_g=(-jnp.exp(ret_decay[0].astype(F32))).reshape(-1),
        attn_scal=attn_scal, btab=btab, lam_init=lam_init,
        diff_norm_g=diff_norm_g[0].reshape(1, DIFF_DV),
        w_out_mix=w_out_mix[0].astype(BF16),
        w_in_conv=w_in_conv[0].astype(BF16),
        conv_w=conv_w[0].reshape(CONV_K, ncb, LANES).transpose(1, 0, 2),
        conv_b=conv_b[0].reshape(1, CONV_W),
        conv_norm_g=conv_norm_g[0].reshape(1, CONV_W),
        conv_norm_b=conv_norm_b[0].reshape(1, CONV_W),
        w_out_conv=w_out_conv[0].astype(BF16),
    )
    return (_trunk(x_prompt, p), _trunk(x_sample, p))
```

```python
import functools
import math

import numpy as np
import jax
import jax.numpy as jnp
from jax import lax
from jax.experimental import pallas as pl
from jax.experimental.pallas import tpu as pltpu

F32 = jnp.float32
BF16 = jnp.bfloat16

D_MODEL = 1024
N_RET_HEADS = 4
RET_DK = 128
RET_DV = 256
N_DIFF_HEADS = 8
DIFF_DH = 64
DIFF_DV = 2 * DIFF_DH
RET_W = N_RET_HEADS * RET_DV
DIFF_W = N_DIFF_HEADS * DIFF_DV
CONV_W = 2 * D_MODEL
CONV_K = 31
CONV_PAD = CONV_K // 2
T5_BUCKETS = 32
T5_MAX_DIST = 128
ROPE_BASE = 10000.0
EPS = 1e-6

_C_RQ = 0
_C_RK = _C_RQ + N_RET_HEADS * RET_DK
_C_RV = _C_RK + N_RET_HEADS * RET_DK
_C_RG = _C_RV + RET_W
_C_DQ = _C_RG + RET_W
_C_DK = _C_DQ + DIFF_W
_C_DV = _C_DK + DIFF_W
_C_DG = _C_DV + DIFF_W
EVEN_IN_W = _C_DG + DIFF_W

LANES = 128
HALO = 16
VMEM_LIMIT = 56 * 1024 * 1024

ROW_TILE = 256
MID_TILE = 256
RET_CHUNK = 256
RET_BLOCK = 1024
ATT_TILE = 256
ATT_QBLOCKS = 4
ATT_ROWS = 32
ATT_DEPTH = 4
ONES_ROWS = 16
LOG2E = math.log2(math.e)
CONV_TILE = 512
CONV_ROWS = 64

M_INIT = -1e30


def _rms(x, g):
    return x * lax.rsqrt(jnp.mean(x * x, axis=-1, keepdims=True) + EPS) * g


def _silu(x):
    return x * jax.nn.sigmoid(x)


def _params(sem):
    return pltpu.CompilerParams(dimension_semantics=sem, vmem_limit_bytes=VMEM_LIMIT)


def _whole(shape):
    nd = len(shape)
    return pl.BlockSpec(shape, lambda *_: (0,) * nd, pipeline_mode=pl.Buffered(1))


def _even_inproj_kernel(x_ref, g_ref, w_ref, cos_ref, sin_ref,
                        rq_ref, rk_ref, rv_ref, rg_ref, dq_ref, dk_ref, dvt_ref, dg_ref):
    hb = _rms(x_ref[...], g_ref[...]).astype(BF16)

    def proj(c0, n):
        return jnp.dot(hb, w_ref[:, c0:c0 + n], preferred_element_type=F32)

    cos = cos_ref[...]
    sin = sin_ref[...]

    def rope(u):
        return u * cos + pltpu.roll(u, RET_DK // 2, 1) * sin

    uq = proj(_C_RQ, N_RET_HEADS * RET_DK)
    uk = proj(_C_RK, N_RET_HEADS * RET_DK)
    for j in range(N_RET_HEADS):
        sl = slice(j * RET_DK, (j + 1) * RET_DK)
        rq_ref[:, sl] = rope(uq[:, sl]).astype(BF16)
        rk_ref[:, sl] = (rope(uk[:, sl]) * (RET_DK ** -0.5)).astype(BF16)
    rv_ref[...] = proj(_C_RV, RET_W).astype(BF16)
    rg_ref[...] = proj(_C_RG, RET_W)
    dq_ref[...] = (proj(_C_DQ, DIFF_W) * (DIFF_DH ** -0.5 * LOG2E)).astype(BF16)
    dk_ref[...] = proj(_C_DK, DIFF_W).astype(BF16)
    dg_ref[...] = proj(_C_DG, DIFF_W)
    dvt_ref[...] = proj(_C_DV, DIFF_W).T.astype(BF16)


def _even_inproj(x, g, w, cos, sin):
    B, S, D = x.shape
    tm = ROW_TILE
    row = lambda n: pl.BlockSpec((None, tm, n), lambda b, i: (b, i, 0))
    tab = pl.BlockSpec((tm, LANES), lambda b, i: (i, 0))
    out_shape = (
        jax.ShapeDtypeStruct((B, S, N_RET_HEADS * RET_DK), BF16),
        jax.ShapeDtypeStruct((B, S, N_RET_HEADS * RET_DK), BF16),
        jax.ShapeDtypeStruct((B, S, RET_W), BF16),
        jax.ShapeDtypeStruct((B, S, RET_W), F32),
        jax.ShapeDtypeStruct((B, S, DIFF_W), BF16),
        jax.ShapeDtypeStruct((B, S, DIFF_W), BF16),
        jax.ShapeDtypeStruct((B, S // ATT_TILE, DIFF_W, ATT_TILE), BF16),
        jax.ShapeDtypeStruct((B, S, DIFF_W), F32),
    )
    per_tile = ATT_TILE // tm
    out_specs = (row(N_RET_HEADS * RET_DK), row(N_RET_HEADS * RET_DK), row(RET_W), row(RET_W),
                 row(DIFF_W), row(DIFF_W),
                 pl.BlockSpec((None, None, DIFF_W, tm), lambda b, i: (b, i // per_tile, 0, i % per_tile)),
                 row(DIFF_W))
    return pl.pallas_call(
        _even_inproj_kernel,
        grid=(B, S // tm),
        in_specs=[row(D), _whole((1, D)), _whole(w.shape), tab, tab],
        out_specs=out_specs,
        out_shape=out_shape,
        compiler_params=_params(("parallel", "parallel")),
        name="even_inproj",
    )(x, g, w, cos, sin)


def _retention_kernel(lg_ref, q_ref, k_ref, v_ref, g_ref, o_ref, lst_ref, st_ref, dmat_ref):
    n = pl.program_id(1)
    nblk = pl.num_programs(1) // 2
    C = RET_CHUNK
    cpb = RET_BLOCK // C
    heads = range(N_RET_HEADS)
    lf = [lg_ref[h] for h in heads]
    lb = [lg_ref[N_RET_HEADS + h] for h in heads]
    pos = lax.broadcasted_iota(jnp.int32, (C, 1), 0).astype(F32)

    def kcols(h):
        return slice(h * RET_DK, (h + 1) * RET_DK)

    def vcols(h):
        return slice(h * RET_DV, (h + 1) * RET_DV)

    def decay(x):
        return jnp.exp(jnp.full((1, RET_DV), x, F32))

    def kt_v(h, rows, w_col):
        kw = (k_ref[rows, kcols(h)].astype(F32) * w_col).T.astype(BF16)
        return jnp.dot(kw, v_ref[rows, vcols(h)], preferred_element_type=F32)

    @pl.when(n == 0)
    def _():
        r = lax.broadcasted_iota(jnp.int32, (C, C), 0)
        c = lax.broadcasted_iota(jnp.int32, (C, C), 1)
        d = (r - c).astype(F32)
        for h in heads:
            dmat_ref[h] = jnp.where(d >= 0, jnp.exp(lf[h] * jnp.maximum(d, 0.0)),
                                    jnp.exp(lb[h] * jnp.maximum(-d, 0.0)))

    @pl.when((n == 0) | (n == nblk))
    def _():
        st_ref[...] = jnp.zeros_like(st_ref)

    @pl.when(n < nblk)
    def _():
        blk = nblk - 1 - n
        for cc in reversed(range(cpb)):
            rows = slice(cc * C, (cc + 1) * C)
            chunk = blk * cpb + cc
            upd = [kt_v(h, rows, jnp.exp(lb[h] * pos)) for h in heads]
            for h in heads:
                lst_ref[chunk, h] = st_ref[h].astype(BF16)
                st_ref[h] = decay(lb[h] * C) * st_ref[h] + upd[h]

    @pl.when(n >= nblk)
    def _():
        blk = n - nblk
        for cc in range(cpb):
            rows = slice(cc * C, (cc + 1) * C)
            chunk = blk * cpb + cc
            q = [q_ref[rows, kcols(h)] for h in heads]
            s = [lax.dot_general(q[h], k_ref[rows, kcols(h)], (((1,), (1,)), ((), ())),
                                 preferred_element_type=F32) for h in heads]
            qf = [(q[h].astype(F32) * jnp.exp(lf[h] * (pos + 1.0))).astype(BF16) for h in heads]
            qb = [(q[h].astype(F32) * jnp.exp(lb[h] * (C - pos))).astype(BF16) for h in heads]
            sd = [(s[h] * dmat_ref[h]).astype(BF16) for h in heads]
            o = [jnp.dot(sd[h], v_ref[rows, vcols(h)], preferred_element_type=F32)
                 + jnp.dot(qf[h], st_ref[h].astype(BF16), preferred_element_type=F32)
                 + jnp.dot(qb[h], lst_ref[chunk, h], preferred_element_type=F32) for h in heads]
            upd = [kt_v(h, rows, jnp.exp(lf[h] * (C - 1.0 - pos))) for h in heads]
            for h in heads:
                st_ref[h] = decay(lf[h] * C) * st_ref[h] + upd[h]
                oh = o[h] * lax.rsqrt(jnp.mean(o[h] * o[h], axis=-1, keepdims=True) + EPS)
                o_ref[rows, vcols(h)] = (oh * _silu(g_ref[rows, vcols(h)])).astype(BF16)


def _retention(log_g, rq, rk, rv, rg):
    B, S, _ = rq.shape
    RB = RET_BLOCK
    NB = S // RB
    KW = N_RET_HEADS * RET_DK

    def both(n):
        return jnp.where(n < NB, NB - 1 - n, n - NB)

    def fwd(n):
        return jnp.maximum(n - NB, 0)

    q_spec = pl.BlockSpec((None, RB, KW), lambda b, n, lg: (b, fwd(n), 0))
    k_spec = pl.BlockSpec((None, RB, KW), lambda b, n, lg: (b, both(n), 0))
    v_spec = pl.BlockSpec((None, RB, RET_W), lambda b, n, lg: (b, both(n), 0))
    fwd_spec = pl.BlockSpec((None, RB, RET_W), lambda b, n, lg: (b, fwd(n), 0))
    grid_spec = pltpu.PrefetchScalarGridSpec(
        num_scalar_prefetch=1,
        grid=(B, 2 * NB),
        in_specs=[q_spec, k_spec, v_spec, fwd_spec],
        out_specs=fwd_spec,
        scratch_shapes=[pltpu.VMEM((S // RET_CHUNK, N_RET_HEADS, RET_DK, RET_DV), BF16),
                        pltpu.VMEM((N_RET_HEADS, RET_DK, RET_DV), F32),
                        pltpu.VMEM((N_RET_HEADS, RET_CHUNK, RET_CHUNK), F32)],
    )
    return pl.pallas_call(
        _retention_kernel,
        grid_spec=grid_spec,
        out_shape=jax.ShapeDtypeStruct((B, S, RET_W), BF16),
        compiler_params=_params(("parallel", "arbitrary")),
        name="retention",
    )(log_g, rq, rk, rv, rg)


def _diff_attn_kernel(sc_ref, q_ref, k_ref, vt_ref, btab_ref, g_ref, ng_ref, o_ref,
                      qz_ref, bias_ref, m_ref, acc_ref, s0_ref, p_ref, *, lam_init):
    h = pl.program_id(1)
    qi = pl.program_id(2)
    T = ATT_TILE
    nk = vt_ref.shape[0]
    lam = sc_ref[0]
    bias_left = sc_ref[1 + h]
    bias_right = sc_ref[1 + N_DIFF_HEADS + h]

    @pl.when(qi == 0)
    def _():
        for d in range(3):
            x = jnp.broadcast_to(btab_ref[d], (T, 2 * T))
            bias_ref[d] = pltpu.roll(x, 0, 1, stride=1, stride_axis=0)[:, :T]

    qt = q_ref[...].astype(F32).T
    half = lax.broadcasted_iota(jnp.int32, qt.shape, 0) < DIFF_DH
    qz_ref[0] = jnp.where(half, qt, 0.0).astype(BF16)
    qz_ref[1] = jnp.where(half, 0.0, qt).astype(BF16)
    m_ref[...] = jnp.full_like(m_ref, M_INIT)
    acc_ref[...] = jnp.zeros_like(acc_ref)
    ones = jnp.ones((ONES_ROWS, T), BF16)

    chains = [(ko, qb, c) for ko in range(ATT_QBLOCKS) for qb in range(ATT_QBLOCKS) for c in range(2)]

    def qcols(qb):
        return slice(qb * T, (qb + 1) * T)

    def scores(chain, kp):
        ko, qb, c = chain
        row0 = pl.multiple_of((kp * ATT_QBLOCKS + ko) * T, T)
        k = k_ref[pl.ds(row0, T), :]
        return jnp.dot(k, qz_ref[c, :, qcols(qb)], preferred_element_type=F32)

    def biased_rows(s, d):
        def rows(r):
            sr = s[r:r + ATT_ROWS]
            return sr if d is None else sr + bias_ref[d, r:r + ATT_ROWS, :]
        return rows

    def running_max(mx, x):
        return x if mx is None else jnp.maximum(mx, x)

    def softmax_pv(chain, vt, rows, mx, next_rows, p_ref):
        _, qb, c = chain
        cols = qcols(qb)
        m_old = m_ref[c, :, cols]
        m_new = jnp.maximum(m_old, jnp.max(mx, axis=0, keepdims=True))
        mx_next = None
        for r in range(0, T, ATT_ROWS):
            p_ref[r:r + ATT_ROWS, :] = jnp.exp2(rows(r) - m_new).astype(BF16)
            if next_rows is not None:
                mx_next = running_max(mx_next, next_rows(r))
        acc_ref[c, :, cols] = (jnp.exp2(m_old - m_new) * acc_ref[c, :, cols]
                               + jnp.dot(vt, p_ref[...], preferred_element_type=F32))
        m_ref[c, :, cols] = m_new
        return mx_next

    def step(kp, offset):
        vts = [jnp.concatenate([vt_ref[kp * ATT_QBLOCKS + ko], ones], axis=0)
               for ko in range(ATT_QBLOCKS)]
        kpn = jnp.minimum(kp + 1, nsteps - 1)

        def block_delta(chain):
            ko, qb, _ = chain
            return None if offset is None else offset * ATT_QBLOCKS + ko - qb

        def rows_of(i):
            delta = block_delta(chains[i])
            return biased_rows(s[i], delta + 1 if delta in (-1, 0, 1) else None)

        s = [s0_ref[i] for i in range(ATT_DEPTH)]
        rows = rows_of(0)
        mx = None
        for r in range(0, T, ATT_ROWS):
            mx = running_max(mx, rows(r))
        for i, chain in enumerate(chains):
            ko, qb, c = chain
            ahead = i + ATT_DEPTH
            if ahead < len(chains):
                s.append(scores(chains[ahead], kp))
            else:
                s0_ref[ahead - len(chains)] = scores(chains[ahead - len(chains)], kpn)
            delta = block_delta(chain)
            if c == 0 and delta == -1:
                m_ref[:, :, qcols(qb)] = m_ref[:, :, qcols(qb)] + bias_left
            if c == 0 and delta == 2:
                m_ref[:, :, qcols(qb)] = m_ref[:, :, qcols(qb)] - bias_right
            next_rows = rows_of(i + 1) if i + 1 < len(chains) else None
            mx = softmax_pv(chain, vts[ko], rows, mx, next_rows, p_ref.at[i % 2])
            rows = next_rows

    nsteps = nk // ATT_QBLOCKS
    for i in range(ATT_DEPTH):
        s0_ref[i] = scores(chains[i], 0)
    n_left = jnp.maximum(qi - 1, 0)

    def far_pair_left(i, carry):
        step(2 * i, None)
        step(2 * i + 1, None)
        return carry

    lax.fori_loop(0, n_left // 2, far_pair_left, 0)

    @pl.when(n_left % 2 == 1)
    def _():
        step(n_left - 1, None)

    for offset in (-1, 0, 1):
        kp = qi + offset

        @pl.when((kp >= 0) & (kp < nsteps))
        def _():
            step(jnp.clip(kp, 0, nsteps - 1), offset)

    first_right = qi + 2
    n_right = jnp.maximum(nsteps - first_right, 0)

    @pl.when(n_right % 2 == 1)
    def _():
        step(jnp.minimum(first_right, nsteps - 1), None)

    paired_right = first_right + n_right % 2

    def far_pair_right(i, carry):
        step(paired_right + 2 * i, None)
        step(paired_right + 2 * i + 1, None)
        return carry

    lax.fori_loop(0, n_right // 2, far_pair_right, 0)

    def normalised(c):
        return acc_ref[c, :DIFF_DV, :] / acc_ref[c, DIFF_DV:DIFF_DV + 1, :]

    ot = normalised(0) - lam * normalised(1)
    o = _rms(ot.T, ng_ref[...]) * (1.0 - lam_init)
    o_ref[...] = (o * _silu(g_ref[...])).astype(BF16)


def _diff_attn(scal, dq, dk, dvt, btab, dg, ng, lam_init):
    B, S, _ = dq.shape
    T = ATT_TILE
    TQ = ATT_QBLOCKS * T
    q_spec = pl.BlockSpec((None, TQ, DIFF_DV), lambda b, h, qi, sc: (b, qi, h))
    k_spec = pl.BlockSpec((None, S, DIFF_DV), lambda b, h, qi, sc: (b, 0, h))
    vt_spec = pl.BlockSpec((None, S // T, DIFF_DV, T), lambda b, h, qi, sc: (b, 0, h, 0))
    btab_spec = pl.BlockSpec((None, 3, 1, 2 * T), lambda b, h, qi, sc: (h, 0, 0, 0))
    ng_spec = pl.BlockSpec((1, DIFF_DV), lambda b, h, qi, sc: (0, 0))
    grid_spec = pltpu.PrefetchScalarGridSpec(
        num_scalar_prefetch=1,
        grid=(B, N_DIFF_HEADS, S // TQ),
        in_specs=[q_spec, k_spec, vt_spec, btab_spec, q_spec, ng_spec],
        out_specs=q_spec,
        scratch_shapes=[pltpu.VMEM((2, DIFF_DV, TQ), BF16),
                        pltpu.VMEM((3, T, T), F32),
                        pltpu.VMEM((2, 1, TQ), F32),
                        pltpu.VMEM((2, DIFF_DV + ONES_ROWS, TQ), F32),
                        pltpu.VMEM((ATT_DEPTH, T, T), F32),
                        pltpu.VMEM((2, T, T), BF16)],
    )
    return pl.pallas_call(
        functools.partial(_diff_attn_kernel, lam_init=lam_init),
        grid_spec=grid_spec,
        out_shape=jax.ShapeDtypeStruct((B, S, DIFF_W), BF16),
        compiler_params=_params(("parallel", "parallel", "arbitrary")),
        name="diff_attn",
    )(scal, dq, dk, dvt, btab, dg, ng)


def _mid_proj_kernel(x_ref, ro_ref, do_ref, wo_ref, g_ref, wi_ref, x1_ref, glu_ref, sg_ref):
    x1 = (x_ref[...]
          + jnp.dot(ro_ref[...], wo_ref[:RET_W, :], preferred_element_type=F32)
          + jnp.dot(do_ref[...], wo_ref[RET_W:, :], preferred_element_type=F32))
    x1_ref[...] = x1
    hb = _rms(x1, g_ref[...]).astype(BF16)
    step = 512
    for c0 in range(0, CONV_W, step):
        a = jnp.dot(hb, wi_ref[:, c0:c0 + step], preferred_element_type=F32)
        b = jnp.dot(hb, wi_ref[:, CONV_W + c0:CONV_W + c0 + step], preferred_element_type=F32)
        gate = jnp.dot(hb, wi_ref[:, 2 * CONV_W + c0:2 * CONV_W + c0 + step], preferred_element_type=F32)
        glu_ref[:, c0:c0 + step] = a * jax.nn.sigmoid(b)
        sg_ref[:, c0:c0 + step] = _silu(gate)


def _mid_proj(x, ro, do, wo, g, wi):
    B, S, D = x.shape
    tm = MID_TILE
    row = lambda n: pl.BlockSpec((None, tm, n), lambda b, i: (b, i, 0))
    return pl.pallas_call(
        _mid_proj_kernel,
        grid=(B, S // tm),
        in_specs=[row(D), row(RET_W), row(DIFF_W), _whole(wo.shape), _whole((1, D)), _whole(wi.shape)],
        out_specs=(row(D), row(CONV_W), row(CONV_W)),
        out_shape=(jax.ShapeDtypeStruct((B, S, D), F32),
                   jax.ShapeDtypeStruct((B, S, CONV_W), F32),
                   jax.ShapeDtypeStruct((B, S, CONV_W), F32)),
        compiler_params=_params(("parallel", "parallel")),
        name="mid_proj",
    )(x, ro, do, wo, g, wi)


def _conv_out_kernel(glu_ref, prev_ref, next_ref, sg_ref, x1_ref, cw_ref, cb_ref, ng_ref, nb_ref,
                     wo_ref, fg_ref, o_ref, win_ref, y_ref):
    i = pl.program_id(1)
    nblk = pl.num_programs(1)
    TS = CONV_TILE
    ncb = CONV_W // LANES
    prev = jnp.where(i > 0, prev_ref[...], 0.0)
    nxt = jnp.where(i < nblk - 1, next_ref[...], 0.0)
    for cb in range(ncb):
        sl = slice(cb * LANES, (cb + 1) * LANES)
        win_ref[cb, 0:HALO, :] = prev[:, sl]
        win_ref[cb, HALO:HALO + TS, :] = glu_ref[:, sl]
        win_ref[cb, HALO + TS:HALO + TS + HALO, :] = nxt[:, sl]

    def chan_block(cb, carry):
        w = cw_ref[cb]
        for r0 in range(0, TS, CONV_ROWS):
            acc = jnp.zeros((CONV_ROWS, LANES), F32)
            for t in range(CONV_K):
                off = r0 + HALO - CONV_PAD + t
                acc = acc + win_ref[cb, pl.ds(off, CONV_ROWS), :] * w[t:t + 1, :]
            y_ref[cb, r0:r0 + CONV_ROWS, :] = acc
        return carry

    lax.fori_loop(0, ncb, chan_block, 0)

    y = jnp.concatenate([y_ref[cb] for cb in range(ncb)], axis=-1) + cb_ref[...]
    mu = jnp.mean(y, axis=-1, keepdims=True)
    yc = y - mu
    var = jnp.mean(yc * yc, axis=-1, keepdims=True)
    yn = yc * lax.rsqrt(var + EPS) * ng_ref[...] + nb_ref[...]
    z = (_silu(yn) * sg_ref[...]).astype(BF16)
    out = x1_ref[...] + jnp.dot(z, wo_ref[...], preferred_element_type=F32)
    o_ref[...] = _rms(out, fg_ref[...])


def _conv_out(glu, sg, x1, cw, cb, ng, nb, wo, fg):
    B, S, D = x1.shape
    TS = CONV_TILE
    hb = TS // HALO
    nh = S // HALO
    row = lambda n: pl.BlockSpec((None, TS, n), lambda b, i: (b, i, 0))
    prev_spec = pl.BlockSpec((None, HALO, CONV_W), lambda b, i: (b, jnp.maximum(i * hb - 1, 0), 0))
    next_spec = pl.BlockSpec((None, HALO, CONV_W), lambda b, i: (b, jnp.minimum((i + 1) * hb, nh - 1), 0))
    ncb = CONV_W // LANES
    return pl.pallas_call(
        _conv_out_kernel,
        grid=(B, S // TS),
        in_specs=[row(CONV_W), prev_spec, next_spec, row(CONV_W), row(D),
                  _whole(cw.shape), _whole((1, CONV_W)), _whole((1, CONV_W)), _whole((1, CONV_W)),
                  _whole(wo.shape), _whole((1, D))],
        out_specs=row(D),
        out_shape=jax.ShapeDtypeStruct((B, S, D), F32),
        scratch_shapes=[pltpu.VMEM((ncb, TS + 2 * HALO, LANES), F32),
                        pltpu.VMEM((ncb, TS, LANES), F32)],
        compiler_params=_params(("parallel", "parallel")),
        name="conv_out",
    )(glu, glu, glu, sg, x1, cw, cb, ng, nb, wo, fg)


def _t5_bucket_np(rel):
    nb = T5_BUCKETS // 2
    max_exact = nb // 2
    ret = (rel > 0).astype(np.int64) * nb
    n = np.abs(rel)
    nf = np.maximum(n, 1).astype(np.float64)
    large = max_exact + (np.log(nf / max_exact) / math.log(T5_MAX_DIST / max_exact)
                         * (nb - max_exact)).astype(np.int64)
    large = np.minimum(large, nb - 1)
    return ret + np.where(n < max_exact, n, large)


def _rope_tables(S):
    half = RET_DK // 2
    inv = 1.0 / (ROPE_BASE ** (jnp.arange(0, RET_DK, 2, dtype=F32) / RET_DK))
    ang = jnp.arange(S).astype(F32)[:, None] * inv[None, :]
    cos = jnp.cos(ang)
    sin = jnp.sin(ang)
    assert cos.shape == (S, half)
    return jnp.concatenate([cos, cos], axis=-1), jnp.concatenate([-sin, sin], axis=-1)


def _trunk(x, p):
    B, S, D = x.shape
    assert D == D_MODEL and ATT_TILE % ROW_TILE == 0
    assert all(S % t == 0 for t in (ATT_TILE * ATT_QBLOCKS, RET_BLOCK, ROW_TILE, MID_TILE, CONV_TILE))

    cos, sin = _rope_tables(S)
    rq, rk, rv, rg, dq, dk, dvt, dg = _even_inproj(x, p["g0"], p["w_in_mix"], cos, sin)
    ro = _retention(p["log_g"], rq, rk, rv, rg)
    do = _diff_attn(p["attn_scal"], dq, dk, dvt, p["btab"], dg, p["diff_norm_g"], p["lam_init"])
    x1, glu, sg = _mid_proj(x, ro, do, p["w_out_mix"], p["g1"], p["w_in_conv"])
    return _conv_out(glu, sg, x1, p["conv_w"], p["conv_b"], p["conv_norm_g"], p["conv_norm_b"],
                     p["w_out_conv"], p["final_g"])


def kernel(x_prompt, x_sample, norm_g, final_g, rel_bias, w_in_mix, ret_decay, lam_q1, lam_k1, lam_q2,
           lam_k2, diff_norm_g, w_out_mix, w_in_conv, conv_w, conv_b, conv_norm_g, conv_norm_b,
           w_out_conv):
    assert norm_g.shape[0] == 2 and w_in_mix.shape[0] == 1 and w_in_conv.shape[0] == 1
    lam_init = 0.8 - 0.6 * math.exp(-0.3 * 0)
    lam = (jnp.exp(jnp.sum(lam_q1[0].astype(F32) * lam_k1[0].astype(F32)))
           - jnp.exp(jnp.sum(lam_q2[0].astype(F32) * lam_k2[0].astype(F32))) + lam_init)
    T = ATT_TILE
    i = np.arange(2 * T)
    idx = np.stack([_t5_bucket_np(np.where(i <= T, off * T - i, off * T + 2 * T - i))
                    for off in (-1, 0, 1)]).astype(np.int32)
    bias2 = rel_bias.astype(F32) * LOG2E
    btab = jnp.take(bias2.T, jnp.asarray(idx), axis=1).reshape(N_DIFF_HEADS, 3, 1, 2 * T)
    far_left = int(_t5_bucket_np(np.array(-T5_MAX_DIST)))
    far_right = int(_t5_bucket_np(np.array(T5_MAX_DIST)))
    attn_scal = jnp.concatenate([lam.reshape(1), bias2[far_left], bias2[far_right]])
    wim = w_in_mix[0]
    ncb = CONV_W // LANES
    p = dict(
        g0=norm_g[0].reshape(1, D_MODEL), g1=norm_g[1].reshape(1, D_MODEL),
        final_g=final_g.reshape(1, D_MODEL),
        w_in_mix=wim.astype(BF16),
        log_g=(-jnp.exp(ret_decay[0].astype(F32))).reshape(-1),
        attn_scal=attn_scal, btab=btab, lam_init=lam_init,
        diff_norm_g=diff_norm_g[0].reshape(1, DIFF_DV),
        w_out_mix=w_out_mix[0].astype(BF16),
        w_in_conv=w_in_conv[0].astype(BF16),
        conv_w=conv_w[0].reshape(CONV_K, ncb, LANES).transpose(1, 0, 2),
        conv_b=conv_b[0].reshape(1, CONV_W),
        conv_norm_g=conv_norm_g[0].reshape(1, CONV_W),
        conv_norm_b=conv_norm_b[0].reshape(1, CONV_W),
        w_out_conv=w_out_conv[0].astype(BF16),
    )
    return (_trunk(x_prompt, p), _trunk(x_sample, p))
```

```python
import functools
import math

import numpy as np
import jax
import jax.numpy as jnp
from jax import lax
from jax.experimental import pallas as pl
from jax.experimental.pallas import tpu as pltpu

F32 = jnp.float32
BF16 = jnp.bfloat16

D_MODEL = 1024
N_RET_HEADS = 4
RET_DK = 128
RET_DV = 256
N_DIFF_HEADS = 8
DIFF_DH = 64
DIFF_DV = 2 * DIFF_DH
RET_W = N_RET_HEADS * RET_DV
DIFF_W = N_DIFF_HEADS * DIFF_DV
CONV_W = 2 * D_MODEL
CONV_K = 31
CONV_PAD = CONV_K // 2
T5_BUCKETS = 32
T5_MAX_DIST = 128
ROPE_BASE = 10000.0
EPS = 1e-6

_C_RQ = 0
_C_RK = _C_RQ + N_RET_HEADS * RET_DK
_C_RV = _C_RK + N_RET_HEADS * RET_DK
_C_RG = _C_RV + RET_W
_C_DQ = _C_RG + RET_W
_C_DK = _C_DQ + DIFF_W
_C_DV = _C_DK + DIFF_W
_C_DG = _C_DV + DIFF_W
EVEN_IN_W = _C_DG + DIFF_W

LANES = 128
HALO = 16
VMEM_LIMIT = 56 * 1024 * 1024

ROW_TILE = 256
MID_TILE = 256
RET_CHUNK = 256
RET_BLOCK = 1024
ATT_TILE = 256
ATT_QBLOCKS = 4
ATT_ROWS = 32
ATT_DEPTH = 4
ATT_PBUFS = 4
ONES_ROWS = 16
LOG2E = math.log2(math.e)
CONV_TILE = 512
CONV_ROWS = 64

M_INIT = -1e30


def _rms(x, g):
    return x * lax.rsqrt(jnp.mean(x * x, axis=-1, keepdims=True) + EPS) * g


def _silu(x):
    return x * jax.nn.sigmoid(x)


def _params(sem):
    return pltpu.CompilerParams(dimension_semantics=sem, vmem_limit_bytes=VMEM_LIMIT)


def _whole(shape):
    nd = len(shape)
    return pl.BlockSpec(shape, lambda *_: (0,) * nd, pipeline_mode=pl.Buffered(1))


def _even_inproj_kernel(x_ref, g_ref, w_ref, cos_ref, sin_ref,
                        rq_ref, rk_ref, rv_ref, rg_ref, dq_ref, dk_ref, dvt_ref, dg_ref):
    hb = _rms(x_ref[...], g_ref[...]).astype(BF16)

    def proj(c0, n):
        return jnp.dot(hb, w_ref[:, c0:c0 + n], preferred_element_type=F32)

    cos = cos_ref[...]
    sin = sin_ref[...]

    def rope(u):
        return u * cos + pltpu.roll(u, RET_DK // 2, 1) * sin

    uq = proj(_C_RQ, N_RET_HEADS * RET_DK)
    uk = proj(_C_RK, N_RET_HEADS * RET_DK)
    for j in range(N_RET_HEADS):
        sl = slice(j * RET_DK, (j + 1) * RET_DK)
        rq_ref[:, sl] = rope(uq[:, sl]).astype(BF16)
        rk_ref[:, sl] = (rope(uk[:, sl]) * (RET_DK ** -0.5)).astype(BF16)
    rv_ref[...] = proj(_C_RV, RET_W).astype(BF16)
    rg_ref[...] = proj(_C_RG, RET_W)
    dq_ref[...] = (proj(_C_DQ, DIFF_W) * (DIFF_DH ** -0.5 * LOG2E)).astype(BF16)
    dk_ref[...] = proj(_C_DK, DIFF_W).astype(BF16)
    dg_ref[...] = proj(_C_DG, DIFF_W)
    dvt_ref[...] = proj(_C_DV, DIFF_W).T.astype(BF16)


def _even_inproj(x, g, w, cos, sin):
    B, S, D = x.shape
    tm = ROW_TILE
    row = lambda n: pl.BlockSpec((None, tm, n), lambda b, i: (b, i, 0))
    tab = pl.BlockSpec((tm, LANES), lambda b, i: (i, 0))
    out_shape = (
        jax.ShapeDtypeStruct((B, S, N_RET_HEADS * RET_DK), BF16),
        jax.ShapeDtypeStruct((B, S, N_RET_HEADS * RET_DK), BF16),
        jax.ShapeDtypeStruct((B, S, RET_W), BF16),
        jax.ShapeDtypeStruct((B, S, RET_W), F32),
        jax.ShapeDtypeStruct((B, S, DIFF_W), BF16),
        jax.ShapeDtypeStruct((B, S, DIFF_W), BF16),
        jax.ShapeDtypeStruct((B, S // ATT_TILE, DIFF_W, ATT_TILE), BF16),
        jax.ShapeDtypeStruct((B, S, DIFF_W), F32),
    )
    per_tile = ATT_TILE // tm
    out_specs = (row(N_RET_HEADS * RET_DK), row(N_RET_HEADS * RET_DK), row(RET_W), row(RET_W),
                 row(DIFF_W), row(DIFF_W),
                 pl.BlockSpec((None, None, DIFF_W, tm), lambda b, i: (b, i // per_tile, 0, i % per_tile)),
                 row(DIFF_W))
    return pl.pallas_call(
        _even_inproj_kernel,
        grid=(B, S // tm),
        in_specs=[row(D), _whole((1, D)), _whole(w.shape), tab, tab],
        out_specs=out_specs,
        out_shape=out_shape,
        compiler_params=_params(("parallel", "parallel")),
        name="even_inproj",
    )(x, g, w, cos, sin)


def _retention_kernel(lg_ref, q_ref, k_ref, v_ref, g_ref, o_ref, lst_ref, st_ref, dmat_ref):
    n = pl.program_id(1)
    nblk = pl.num_programs(1) // 2
    C = RET_CHUNK
    cpb = RET_BLOCK // C
    heads = range(N_RET_HEADS)
    lf = [lg_ref[h] for h in heads]
    lb = [lg_ref[N_RET_HEADS + h] for h in heads]
    pos = lax.broadcasted_iota(jnp.int32, (C, 1), 0).astype(F32)

    def kcols(h):
        return slice(h * RET_DK, (h + 1) * RET_DK)

    def vcols(h):
        return slice(h * RET_DV, (h + 1) * RET_DV)

    def decay(x):
        return jnp.exp(jnp.full((1, RET_DV), x, F32))

    def kt_v(h, rows, w_col):
        kw = (k_ref[rows, kcols(h)].astype(F32) * w_col).T.astype(BF16)
        return jnp.dot(kw, v_ref[rows, vcols(h)], preferred_element_type=F32)

    @pl.when(n == 0)
    def _():
        r = lax.broadcasted_iota(jnp.int32, (C, C), 0)
        c = lax.broadcasted_iota(jnp.int32, (C, C), 1)
        d = (r - c).astype(F32)
        for h in heads:
            dmat_ref[h] = jnp.where(d >= 0, jnp.exp(lf[h] * jnp.maximum(d, 0.0)),
                                    jnp.exp(lb[h] * jnp.maximum(-d, 0.0)))

    @pl.when((n == 0) | (n == nblk))
    def _():
        st_ref[...] = jnp.zeros_like(st_ref)

    @pl.when(n < nblk)
    def _():
        blk = nblk - 1 - n
        for cc in reversed(range(cpb)):
            rows = slice(cc * C, (cc + 1) * C)
            chunk = blk * cpb + cc
            upd = [kt_v(h, rows, jnp.exp(lb[h] * pos)) for h in heads]
            for h in heads:
                lst_ref[chunk, h] = st_ref[h].astype(BF16)
                st_ref[h] = decay(lb[h] * C) * st_ref[h] + upd[h]

    @pl.when(n >= nblk)
    def _():
        blk = n - nblk
        for cc in range(cpb):
            rows = slice(cc * C, (cc + 1) * C)
            chunk = blk * cpb + cc
            q = [q_ref[rows, kcols(h)] for h in heads]
            s = [lax.dot_general(q[h], k_ref[rows, kcols(h)], (((1,), (1,)), ((), ())),
                                 preferred_element_type=F32) for h in heads]
            qf = [(q[h].astype(F32) * jnp.exp(lf[h] * (pos + 1.0))).astype(BF16) for h in heads]
            qb = [(q[h].astype(F32) * jnp.exp(lb[h] * (C - pos))).astype(BF16) for h in heads]
            sd = [(s[h] * dmat_ref[h]).astype(BF16) for h in heads]
            o = [jnp.dot(sd[h], v_ref[rows, vcols(h)], preferred_element_type=F32)
                 + jnp.dot(qf[h], st_ref[h].astype(BF16), preferred_element_type=F32)
                 + jnp.dot(qb[h], lst_ref[chunk, h], preferred_element_type=F32) for h in heads]
            upd = [kt_v(h, rows, jnp.exp(lf[h] * (C - 1.0 - pos))) for h in heads]
            for h in heads:
                st_ref[h] = decay(lf[h] * C) * st_ref[h] + upd[h]
                oh = o[h] * lax.rsqrt(jnp.mean(o[h] * o[h], axis=-1, keepdims=True) + EPS)
                o_ref[rows, vcols(h)] = (oh * _silu(g_ref[rows, vcols(h)])).astype(BF16)


def _retention(log_g, rq, rk, rv, rg):
    B, S, _ = rq.shape
    RB = RET_BLOCK
    NB = S // RB
    KW = N_RET_HEADS * RET_DK

    def both(n):
        return jnp.where(n < NB, NB - 1 - n, n - NB)

    def fwd(n):
        return jnp.maximum(n - NB, 0)

    q_spec = pl.BlockSpec((None, RB, KW), lambda b, n, lg: (b, fwd(n), 0))
    k_spec = pl.BlockSpec((None, RB, KW), lambda b, n, lg: (b, both(n), 0))
    v_spec = pl.BlockSpec((None, RB, RET_W), lambda b, n, lg: (b, both(n), 0))
    fwd_spec = pl.BlockSpec((None, RB, RET_W), lambda b, n, lg: (b, fwd(n), 0))
    grid_spec = pltpu.PrefetchScalarGridSpec(
        num_scalar_prefetch=1,
        grid=(B, 2 * NB),
        in_specs=[q_spec, k_spec, v_spec, fwd_spec],
        out_specs=fwd_spec,
        scratch_shapes=[pltpu.VMEM((S // RET_CHUNK, N_RET_HEADS, RET_DK, RET_DV), BF16),
                        pltpu.VMEM((N_RET_HEADS, RET_DK, RET_DV), F32),
                        pltpu.VMEM((N_RET_HEADS, RET_CHUNK, RET_CHUNK), F32)],
    )
    return pl.pallas_call(
        _retention_kernel,
        grid_spec=grid_spec,
        out_shape=jax.ShapeDtypeStruct((B, S, RET_W), BF16),
        compiler_params=_params(("parallel", "arbitrary")),
        name="retention",
    )(log_g, rq, rk, rv, rg)


def _diff_attn_kernel(sc_ref, q_ref, k_ref, vt_ref, btab_ref, g_ref, ng_ref, o_ref,
                      qz_ref, bias_ref, m_ref, acc_ref, s0_ref, p_ref, *, lam_init):
    h = pl.program_id(1)
    qi = pl.program_id(2)
    T = ATT_TILE
    nk = vt_ref.shape[0]
    lam = sc_ref[0]
    bias_left = sc_ref[1 + h]
    bias_right = sc_ref[1 + N_DIFF_HEADS + h]

    @pl.when(qi == 0)
    def _():
        for d in range(3):
            x = jnp.broadcast_to(btab_ref[d], (T, 2 * T))
            bias_ref[d] = pltpu.roll(x, 0, 1, stride=1, stride_axis=0)[:, :T]

    qt = q_ref[...].astype(F32).T
    half = lax.broadcasted_iota(jnp.int32, qt.shape, 0) < DIFF_DH
    qz_ref[0] = jnp.where(half, qt, 0.0).astype(BF16)
    qz_ref[1] = jnp.where(half, 0.0, qt).astype(BF16)
    m_ref[...] = jnp.full_like(m_ref, M_INIT)
    acc_ref[...] = jnp.zeros_like(acc_ref)
    ones = jnp.ones((ONES_ROWS, T), BF16)

    chains = [(ko, qb, c) for ko in range(ATT_QBLOCKS) for qb in range(ATT_QBLOCKS) for c in range(2)]

    def qcols(qb):
        return slice(qb * T, (qb + 1) * T)

    def scores(chain, kp):
        ko, qb, c = chain
        row0 = pl.multiple_of((kp * ATT_QBLOCKS + ko) * T, T)
        k = k_ref[pl.ds(row0, T), :]
        return jnp.dot(k, qz_ref[c, :, qcols(qb)], preferred_element_type=F32)

    def biased_rows(s, d):
        def rows(r):
            sr = s[r:r + ATT_ROWS]
            return sr if d is None else sr + bias_ref[d, r:r + ATT_ROWS, :]
        return rows

    def running_max(mx, x):
        return x if mx is None else jnp.maximum(mx, x)

    def softmax_pv(chain, vt, rows, mx, next_rows, p_ref):
        _, qb, c = chain
        cols = qcols(qb)
        m_old = m_ref[c, :, cols]
        m_new = jnp.maximum(m_old, jnp.max(mx, axis=0, keepdims=True))
        mx_next = None
        for r in range(0, T, ATT_ROWS):
            p_ref[r:r + ATT_ROWS, :] = jnp.exp2(rows(r) - m_new).astype(BF16)
            if next_rows is not None:
                mx_next = running_max(mx_next, next_rows(r))
        acc_ref[c, :, cols] = (jnp.exp2(m_old - m_new) * acc_ref[c, :, cols]
                               + jnp.dot(vt, p_ref[...], preferred_element_type=F32))
        m_ref[c, :, cols] = m_new
        return mx_next

    def step(kp, offset):
        vts = [jnp.concatenate([vt_ref[kp * ATT_QBLOCKS + ko], ones], axis=0)
               for ko in range(ATT_QBLOCKS)]
        kpn = jnp.minimum(kp + 1, nsteps - 1)

        def block_delta(chain):
            ko, qb, _ = chain
            return None if offset is None else offset * ATT_QBLOCKS + ko - qb

        def rows_of(i):
            delta = block_delta(chains[i])
            return biased_rows(s[i], delta + 1 if delta in (-1, 0, 1) else None)

        s = [s0_ref[i] for i in range(ATT_DEPTH)]
        rows = rows_of(0)
        mx = None
        for r in range(0, T, ATT_ROWS):
            mx = running_max(mx, rows(r))
        for i, chain in enumerate(chains):
            ko, qb, c = chain
            ahead = i + ATT_DEPTH
            if ahead < len(chains):
                s.append(scores(chains[ahead], kp))
            else:
                s0_ref[ahead - len(chains)] = scores(chains[ahead - len(chains)], kpn)
            delta = block_delta(chain)
            if c == 0 and delta == -1:
                m_ref[:, :, qcols(qb)] = m_ref[:, :, qcols(qb)] + bias_left
            if c == 0 and delta == 2:
                m_ref[:, :, qcols(qb)] = m_ref[:, :, qcols(qb)] - bias_right
            next_rows = rows_of(i + 1) if i + 1 < len(chains) else None
            mx = softmax_pv(chain, vts[ko], rows, mx, next_rows, p_ref.at[i % ATT_PBUFS])
            rows = next_rows

    nsteps = nk // ATT_QBLOCKS
    for i in range(ATT_DEPTH):
        s0_ref[i] = scores(chains[i], 0)
    n_left = jnp.maximum(qi - 1, 0)

    def far_pair_left(i, carry):
        step(2 * i, None)
        step(2 * i + 1, None)
        return carry

    lax.fori_loop(0, n_left // 2, far_pair_left, 0)

    @pl.when(n_left % 2 == 1)
    def _():
        step(n_left - 1, None)

    for offset in (-1, 0, 1):
        kp = qi + offset

        @pl.when((kp >= 0) & (kp < nsteps))
        def _():
            step(jnp.clip(kp, 0, nsteps - 1), offset)

    first_right = qi + 2
    n_right = jnp.maximum(nsteps - first_right, 0)

    @pl.when(n_right % 2 == 1)
    def _():
        step(jnp.minimum(first_right, nsteps - 1), None)

    paired_right = first_right + n_right % 2

    def far_pair_right(i, carry):
        step(paired_right + 2 * i, None)
        step(paired_right + 2 * i + 1, None)
        return carry

    lax.fori_loop(0, n_right // 2, far_pair_right, 0)

    def normalised(c):
        return acc_ref[c, :DIFF_DV, :] / acc_ref[c, DIFF_DV:DIFF_DV + 1, :]

    ot = normalised(0) - lam * normalised(1)
    o = _rms(ot.T, ng_ref[...]) * (1.0 - lam_init)
    o_ref[...] = (o * _silu(g_ref[...])).astype(BF16)


def _diff_attn(scal, dq, dk, dvt, btab, dg, ng, lam_init):
    B, S, _ = dq.shape
    T = ATT_TILE
    TQ = ATT_QBLOCKS * T
    q_spec = pl.BlockSpec((None, TQ, DIFF_DV), lambda b, h, qi, sc: (b, qi, h))
    k_spec = pl.BlockSpec((None, S, DIFF_DV), lambda b, h, qi, sc: (b, 0, h))
    vt_spec = pl.BlockSpec((None, S // T, DIFF_DV, T), lambda b, h, qi, sc: (b, 0, h, 0))
    btab_spec = pl.BlockSpec((None, 3, 1, 2 * T), lambda b, h, qi, sc: (h, 0, 0, 0))
    ng_spec = pl.BlockSpec((1, DIFF_DV), lambda b, h, qi, sc: (0, 0))
    grid_spec = pltpu.PrefetchScalarGridSpec(
        num_scalar_prefetch=1,
        grid=(B, N_DIFF_HEADS, S // TQ),
        in_specs=[q_spec, k_spec, vt_spec, btab_spec, q_spec, ng_spec],
        out_specs=q_spec,
        scratch_shapes=[pltpu.VMEM((2, DIFF_DV, TQ), BF16),
                        pltpu.VMEM((3, T, T), F32),
                        pltpu.VMEM((2, 1, TQ), F32),
                        pltpu.VMEM((2, DIFF_DV + ONES_ROWS, TQ), F32),
                        pltpu.VMEM((ATT_DEPTH, T, T), F32),
                        pltpu.VMEM((ATT_PBUFS, T, T), BF16)],
    )
    return pl.pallas_call(
        functools.partial(_diff_attn_kernel, lam_init=lam_init),
        grid_spec=grid_spec,
        out_shape=jax.ShapeDtypeStruct((B, S, DIFF_W), BF16),
        compiler_params=_params(("parallel", "parallel", "arbitrary")),
        name="diff_attn",
    )(scal, dq, dk, dvt, btab, dg, ng)


def _mid_proj_kernel(x_ref, ro_ref, do_ref, wo_ref, g_ref, wi_ref, x1_ref, glu_ref, sg_ref):
    x1 = (x_ref[...]
          + jnp.dot(ro_ref[...], wo_ref[:RET_W, :], preferred_element_type=F32)
          + jnp.dot(do_ref[...], wo_ref[RET_W:, :], preferred_element_type=F32))
    x1_ref[...] = x1
    hb = _rms(x1, g_ref[...]).astype(BF16)
    step = 512
    for c0 in range(0, CONV_W, step):
        a = jnp.dot(hb, wi_ref[:, c0:c0 + step], preferred_element_type=F32)
        b = jnp.dot(hb, wi_ref[:, CONV_W + c0:CONV_W + c0 + step], preferred_element_type=F32)
        gate = jnp.dot(hb, wi_ref[:, 2 * CONV_W + c0:2 * CONV_W + c0 + step], preferred_element_type=F32)
        glu_ref[:, c0:c0 + step] = a * jax.nn.sigmoid(b)
        sg_ref[:, c0:c0 + step] = _silu(gate)


def _mid_proj(x, ro, do, wo, g, wi):
    B, S, D = x.shape
    tm = MID_TILE
    row = lambda n: pl.BlockSpec((None, tm, n), lambda b, i: (b, i, 0))
    return pl.pallas_call(
        _mid_proj_kernel,
        grid=(B, S // tm),
        in_specs=[row(D), row(RET_W), row(DIFF_W), _whole(wo.shape), _whole((1, D)), _whole(wi.shape)],
        out_specs=(row(D), row(CONV_W), row(CONV_W)),
        out_shape=(jax.ShapeDtypeStruct((B, S, D), F32),
                   jax.ShapeDtypeStruct((B, S, CONV_W), F32),
                   jax.ShapeDtypeStruct((B, S, CONV_W), F32)),
        compiler_params=_params(("parallel", "parallel")),
        name="mid_proj",
    )(x, ro, do, wo, g, wi)


def _conv_out_kernel(glu_ref, prev_ref, next_ref, sg_ref, x1_ref, cw_ref, cb_ref, ng_ref, nb_ref,
                     wo_ref, fg_ref, o_ref, win_ref, y_ref):
    i = pl.program_id(1)
    nblk = pl.num_programs(1)
    TS = CONV_TILE
    ncb = CONV_W // LANES
    prev = jnp.where(i > 0, prev_ref[...], 0.0)
    nxt = jnp.where(i < nblk - 1, next_ref[...], 0.0)
    for cb in range(ncb):
        sl = slice(cb * LANES, (cb + 1) * LANES)
        win_ref[cb, 0:HALO, :] = prev[:, sl]
        win_ref[cb, HALO:HALO + TS, :] = glu_ref[:, sl]
        win_ref[cb, HALO + TS:HALO + TS + HALO, :] = nxt[:, sl]

    def chan_block(cb, carry):
        w = cw_ref[cb]
        for r0 in range(0, TS, CONV_ROWS):
            acc = jnp.zeros((CONV_ROWS, LANES), F32)
            for t in range(CONV_K):
                off = r0 + HALO - CONV_PAD + t
                acc = acc + win_ref[cb, pl.ds(off, CONV_ROWS), :] * w[t:t + 1, :]
            y_ref[cb, r0:r0 + CONV_ROWS, :] = acc
        return carry

    lax.fori_loop(0, ncb, chan_block, 0)

    y = jnp.concatenate([y_ref[cb] for cb in range(ncb)], axis=-1) + cb_ref[...]
    mu = jnp.mean(y, axis=-1, keepdims=True)
    yc = y - mu
    var = jnp.mean(yc * yc, axis=-1, keepdims=True)
    yn = yc * lax.rsqrt(var + EPS) * ng_ref[...] + nb_ref[...]
    z = (_silu(yn) * sg_ref[...]).astype(BF16)
    out = x1_ref[...] + jnp.dot(z, wo_ref[...], preferred_element_type=F32)
    o_ref[...] = _rms(out, fg_ref[...])


def _conv_out(glu, sg, x1, cw, cb, ng, nb, wo, fg):
    B, S, D = x1.shape
    TS = CONV_TILE
    hb = TS // HALO
    nh = S // HALO
    row = lambda n: pl.BlockSpec((None, TS, n), lambda b, i: (b, i, 0))
    prev_spec = pl.BlockSpec((None, HALO, CONV_W), lambda b, i: (b, jnp.maximum(i * hb - 1, 0), 0))
    next_spec = pl.BlockSpec((None, HALO, CONV_W), lambda b, i: (b, jnp.minimum((i + 1) * hb, nh - 1), 0))
    ncb = CONV_W // LANES
    return pl.pallas_call(
        _conv_out_kernel,
        grid=(B, S // TS),
        in_specs=[row(CONV_W), prev_spec, next_spec, row(CONV_W), row(D),
                  _whole(cw.shape), _whole((1, CONV_W)), _whole((1, CONV_W)), _whole((1, CONV_W)),
                  _whole(wo.shape), _whole((1, D))],
        out_specs=row(D),
        out_shape=jax.ShapeDtypeStruct((B, S, D), F32),
        scratch_shapes=[pltpu.VMEM((ncb, TS + 2 * HALO, LANES), F32),
                        pltpu.VMEM((ncb, TS, LANES), F32)],
        compiler_params=_params(("parallel", "parallel")),
        name="conv_out",
    )(glu, glu, glu, sg, x1, cw, cb, ng, nb, wo, fg)


def _t5_bucket_np(rel):
    nb = T5_BUCKETS // 2
    max_exact = nb // 2
    ret = (rel > 0).astype(np.int64) * nb
    n = np.abs(rel)
    nf = np.maximum(n, 1).astype(np.float64)
    large = max_exact + (np.log(nf / max_exact) / math.log(T5_MAX_DIST / max_exact)
                         * (nb - max_exact)).astype(np.int64)
    large = np.minimum(large, nb - 1)
    return ret + np.where(n < max_exact, n, large)


def _rope_tables(S):
    half = RET_DK // 2
    inv = 1.0 / (ROPE_BASE ** (jnp.arange(0, RET_DK, 2, dtype=F32) / RET_DK))
    ang = jnp.arange(S).astype(F32)[:, None] * inv[None, :]
    cos = jnp.cos(ang)
    sin = jnp.sin(ang)
    assert cos.shape == (S, half)
    return jnp.concatenate([cos, cos], axis=-1), jnp.concatenate([-sin, sin], axis=-1)


def _trunk(x, p):
    B, S, D = x.shape
    assert D == D_MODEL and ATT_TILE % ROW_TILE == 0
    assert all(S % t == 0 for t in (ATT_TILE * ATT_QBLOCKS, RET_BLOCK, ROW_TILE, MID_TILE, CONV_TILE))

    cos, sin = _rope_tables(S)
    rq, rk, rv, rg, dq, dk, dvt, dg = _even_inproj(x, p["g0"], p["w_in_mix"], cos, sin)
    ro = _retention(p["log_g"], rq, rk, rv, rg)
    do = _diff_attn(p["attn_scal"], dq, dk, dvt, p["btab"], dg, p["diff_norm_g"], p["lam_init"])
    x1, glu, sg = _mid_proj(x, ro, do, p["w_out_mix"], p["g1"], p["w_in_conv"])
    return _conv_out(glu, sg, x1, p["conv_w"], p["conv_b"], p["conv_norm_g"], p["conv_norm_b"],
                     p["w_out_conv"], p["final_g"])


def kernel(x_prompt, x_sample, norm_g, final_g, rel_bias, w_in_mix, ret_decay, lam_q1, lam_k1, lam_q2,
           lam_k2, diff_norm_g, w_out_mix, w_in_conv, conv_w, conv_b, conv_norm_g, conv_norm_b,
           w_out_conv):
    assert norm_g.shape[0] == 2 and w_in_mix.shape[0] == 1 and w_in_conv.shape[0] == 1
    lam_init = 0.8 - 0.6 * math.exp(-0.3 * 0)
    lam = (jnp.exp(jnp.sum(lam_q1[0].astype(F32) * lam_k1[0].astype(F32)))
           - jnp.exp(jnp.sum(lam_q2[0].astype(F32) * lam_k2[0].astype(F32))) + lam_init)
    T = ATT_TILE
    i = np.arange(2 * T)
    idx = np.stack([_t5_bucket_np(np.where(i <= T, off * T - i, off * T + 2 * T - i))
                    for off in (-1, 0, 1)]).astype(np.int32)
    bias2 = rel_bias.astype(F32) * LOG2E
    btab = jnp.take(bias2.T, jnp.asarray(idx), axis=1).reshape(N_DIFF_HEADS, 3, 1, 2 * T)
    far_left = int(_t5_bucket_np(np.array(-T5_MAX_DIST)))
    far_right = int(_t5_bucket_np(np.array(T5_MAX_DIST)))
    attn_scal = jnp.concatenate([lam.reshape(1), bias2[far_left], bias2[far_right]])
    wim = w_in_mix[0]
    ncb = CONV_W // LANES
    p = dict(
        g0=norm_g[0].reshape(1, D_MODEL), g1=norm_g[1].reshape(1, D_MODEL),
        final_g=final_g.reshape(1, D_MODEL),
        w_in_mix=wim.astype(BF16),
        log_g=(-jnp.exp(ret_decay[0].astype(F32))).reshape(-1),
        attn_scal=attn_scal, btab=btab, lam_init=lam_init,
        diff_norm_g=diff_norm_g[0].reshape(1, DIFF_DV),
        w_out_mix=w_out_mix[0].astype(BF16),
        w_in_conv=w_in_conv[0].astype(BF16),
        conv_w=conv_w[0].reshape(CONV_K, ncb, LANES).transpose(1, 0, 2),
        conv_b=conv_b[0].reshape(1, CONV_W),
        conv_norm_g=conv_norm_g[0].reshape(1, CONV_W),
        conv_norm_b=conv_norm_b[0].reshape(1, CONV_W),
        w_out_conv=w_out_conv[0].astype(BF16),
    )
    return (_trunk(x_prompt, p), _trunk(x_sample, p))
```

```python
import functools
import math

import numpy as np
import jax
import jax.numpy as jnp
from jax import lax
from jax.experimental import pallas as pl
from jax.experimental.pallas import tpu as pltpu

F32 = jnp.float32
BF16 = jnp.bfloat16

D_MODEL = 1024
N_RET_HEADS = 4
RET_DK = 128
RET_DV = 256
N_DIFF_HEADS = 8
DIFF_DH = 64
DIFF_DV = 2 * DIFF_DH
RET_W = N_RET_HEADS * RET_DV
DIFF_W = N_DIFF_HEADS * DIFF_DV
CONV_W = 2 * D_MODEL
CONV_K = 31
CONV_PAD = CONV_K // 2
T5_BUCKETS = 32
T5_MAX_DIST = 128
ROPE_BASE = 10000.0
EPS = 1e-6

_C_RQ = 0
_C_RK = _C_RQ + N_RET_HEADS * RET_DK
_C_RV = _C_RK + N_RET_HEADS * RET_DK
_C_RG = _C_RV + RET_W
_C_DQ = _C_RG + RET_W
_C_DK = _C_DQ + DIFF_W
_C_DV = _C_DK + DIFF_W
_C_DG = _C_DV + DIFF_W
EVEN_IN_W = _C_DG + DIFF_W

LANES = 128
HALO = 16
VMEM_LIMIT = 56 * 1024 * 1024

ROW_TILE = 256
MID_TILE = 256
RET_CHUNK = 256
RET_BLOCK = 1024
ATT_TILE = 256
ATT_QBLOCKS = 4
ATT_ROWS = 32
ATT_DEPTH = 4
ATT_PBUFS = 4
ONES_ROWS = 16
LOG2E = math.log2(math.e)
CONV_TILE = 512
CONV_ROWS = 64

M_INIT = -1e30


def _rms(x, g):
    return x * lax.rsqrt(jnp.mean(x * x, axis=-1, keepdims=True) + EPS) * g


def _silu(x):
    return x * jax.nn.sigmoid(x)


def _params(sem):
    return pltpu.CompilerParams(dimension_semantics=sem, vmem_limit_bytes=VMEM_LIMIT)


def _whole(shape):
    nd = len(shape)
    return pl.BlockSpec(shape, lambda *_: (0,) * nd, pipeline_mode=pl.Buffered(1))


def _even_inproj_kernel(x_ref, g_ref, w_ref, cos_ref, sin_ref,
                        rq_ref, rk_ref, rv_ref, rg_ref, dq_ref, dk_ref, dvt_ref, dg_ref):
    hb = _rms(x_ref[...], g_ref[...]).astype(BF16)

    def proj(c0, n):
        return jnp.dot(hb, w_ref[:, c0:c0 + n], preferred_element_type=F32)

    cos = cos_ref[...]
    sin = sin_ref[...]

    def rope(u):
        return u * cos + pltpu.roll(u, RET_DK // 2, 1) * sin

    uq = proj(_C_RQ, N_RET_HEADS * RET_DK)
    uk = proj(_C_RK, N_RET_HEADS * RET_DK)
    for j in range(N_RET_HEADS):
        sl = slice(j * RET_DK, (j + 1) * RET_DK)
        rq_ref[:, sl] = rope(uq[:, sl]).astype(BF16)
        rk_ref[:, sl] = (rope(uk[:, sl]) * (RET_DK ** -0.5)).astype(BF16)
    rv_ref[...] = proj(_C_RV, RET_W).astype(BF16)
    rg_ref[...] = proj(_C_RG, RET_W)
    dq_ref[...] = (proj(_C_DQ, DIFF_W) * (DIFF_DH ** -0.5 * LOG2E)).astype(BF16)
    dk_ref[...] = proj(_C_DK, DIFF_W).astype(BF16)
    dg_ref[...] = proj(_C_DG, DIFF_W)
    dvt_ref[...] = proj(_C_DV, DIFF_W).T.astype(BF16)


def _even_inproj(x, g, w, cos, sin):
    B, S, D = x.shape
    tm = ROW_TILE
    row = lambda n: pl.BlockSpec((None, tm, n), lambda b, i: (b, i, 0))
    tab = pl.BlockSpec((tm, LANES), lambda b, i: (i, 0))
    out_shape = (
        jax.ShapeDtypeStruct((B, S, N_RET_HEADS * RET_DK), BF16),
        jax.ShapeDtypeStruct((B, S, N_RET_HEADS * RET_DK), BF16),
        jax.ShapeDtypeStruct((B, S, RET_W), BF16),
        jax.ShapeDtypeStruct((B, S, RET_W), F32),
        jax.ShapeDtypeStruct((B, S, DIFF_W), BF16),
        jax.ShapeDtypeStruct((B, S, DIFF_W), BF16),
        jax.ShapeDtypeStruct((B, S // ATT_TILE, DIFF_W, ATT_TILE), BF16),
        jax.ShapeDtypeStruct((B, S, DIFF_W), F32),
    )
    per_tile = ATT_TILE // tm
    out_specs = (row(N_RET_HEADS * RET_DK), row(N_RET_HEADS * RET_DK), row(RET_W), row(RET_W),
                 row(DIFF_W), row(DIFF_W),
                 pl.BlockSpec((None, None, DIFF_W, tm), lambda b, i: (b, i // per_tile, 0, i % per_tile)),
                 row(DIFF_W))
    return pl.pallas_call(
        _even_inproj_kernel,
        grid=(B, S // tm),
        in_specs=[row(D), _whole((1, D)), _whole(w.shape), tab, tab],
        out_specs=out_specs,
        out_shape=out_shape,
        compiler_params=_params(("parallel", "parallel")),
        name="even_inproj",
    )(x, g, w, cos, sin)


def _retention_kernel(lg_ref, q_ref, k_ref, v_ref, g_ref, o_ref, lst_ref, st_ref, dmat_ref):
    n = pl.program_id(1)
    nblk = pl.num_programs(1) // 2
    C = RET_CHUNK
    cpb = RET_BLOCK // C
    heads = range(N_RET_HEADS)
    lf = [lg_ref[h] for h in heads]
    lb = [lg_ref[N_RET_HEADS + h] for h in heads]
    pos = lax.broadcasted_iota(jnp.int32, (C, 1), 0).astype(F32)

    def kcols(h):
        return slice(h * RET_DK, (h + 1) * RET_DK)

    def vcols(h):
        return slice(h * RET_DV, (h + 1) * RET_DV)

    def decay(x):
        return jnp.exp(jnp.full((1, RET_DV), x, F32))

    def kt_v(h, rows, w_col):
        kw = (k_ref[rows, kcols(h)].astype(F32) * w_col).T.astype(BF16)
        return jnp.dot(kw, v_ref[rows, vcols(h)], preferred_element_type=F32)

    @pl.when(n == 0)
    def _():
        r = lax.broadcasted_iota(jnp.int32, (C, C), 0)
        c = lax.broadcasted_iota(jnp.int32, (C, C), 1)
        d = (r - c).astype(F32)
        for h in heads:
            dmat_ref[h] = jnp.where(d >= 0, jnp.exp(lf[h] * jnp.maximum(d, 0.0)),
                                    jnp.exp(lb[h] * jnp.maximum(-d, 0.0)))

    @pl.when((n == 0) | (n == nblk))
    def _():
        st_ref[...] = jnp.zeros_like(st_ref)

    @pl.when(n < nblk)
    def _():
        blk = nblk - 1 - n
        for cc in reversed(range(cpb)):
            rows = slice(cc * C, (cc + 1) * C)
            chunk = blk * cpb + cc
            upd = [kt_v(h, rows, jnp.exp(lb[h] * pos)) for h in heads]
            for h in heads:
                lst_ref[chunk, h] = st_ref[h].astype(BF16)
                st_ref[h] = decay(lb[h] * C) * st_ref[h] + upd[h]

    @pl.when(n >= nblk)
    def _():
        blk = n - nblk
        for cc in range(cpb):
            rows = slice(cc * C, (cc + 1) * C)
            chunk = blk * cpb + cc
            q = [q_ref[rows, kcols(h)] for h in heads]
            s = [lax.dot_general(q[h], k_ref[rows, kcols(h)], (((1,), (1,)), ((), ())),
                                 preferred_element_type=F32) for h in heads]
            qf = [(q[h].astype(F32) * jnp.exp(lf[h] * (pos + 1.0))).astype(BF16) for h in heads]
            qb = [(q[h].astype(F32) * jnp.exp(lb[h] * (C - pos))).astype(BF16) for h in heads]
            sd = [(s[h] * dmat_ref[h]).astype(BF16) for h in heads]
            o = [jnp.dot(sd[h], v_ref[rows, vcols(h)], preferred_element_type=F32)
                 + jnp.dot(qf[h], st_ref[h].astype(BF16), preferred_element_type=F32)
                 + jnp.dot(qb[h], lst_ref[chunk, h], preferred_element_type=F32) for h in heads]
            upd = [kt_v(h, rows, jnp.exp(lf[h] * (C - 1.0 - pos))) for h in heads]
            for h in heads:
                st_ref[h] = decay(lf[h] * C) * st_ref[h] + upd[h]
                oh = o[h] * lax.rsqrt(jnp.mean(o[h] * o[h], axis=-1, keepdims=True) + EPS)
                o_ref[rows, vcols(h)] = (oh * _silu(g_ref[rows, vcols(h)])).astype(BF16)


def _retention(log_g, rq, rk, rv, rg):
    B, S, _ = rq.shape
    RB = RET_BLOCK
    NB = S // RB
    KW = N_RET_HEADS * RET_DK

    def both(n):
        return jnp.where(n < NB, NB - 1 - n, n - NB)

    def fwd(n):
        return jnp.maximum(n - NB, 0)

    q_spec = pl.BlockSpec((None, RB, KW), lambda b, n, lg: (b, fwd(n), 0))
    k_spec = pl.BlockSpec((None, RB, KW), lambda b, n, lg: (b, both(n), 0))
    v_spec = pl.BlockSpec((None, RB, RET_W), lambda b, n, lg: (b, both(n), 0))
    fwd_spec = pl.BlockSpec((None, RB, RET_W), lambda b, n, lg: (b, fwd(n), 0))
    grid_spec = pltpu.PrefetchScalarGridSpec(
        num_scalar_prefetch=1,
        grid=(B, 2 * NB),
        in_specs=[q_spec, k_spec, v_spec, fwd_spec],
        out_specs=fwd_spec,
        scratch_shapes=[pltpu.VMEM((S // RET_CHUNK, N_RET_HEADS, RET_DK, RET_DV), BF16),
                        pltpu.VMEM((N_RET_HEADS, RET_DK, RET_DV), F32),
                        pltpu.VMEM((N_RET_HEADS, RET_CHUNK, RET_CHUNK), F32)],
    )
    return pl.pallas_call(
        _retention_kernel,
        grid_spec=grid_spec,
        out_shape=jax.ShapeDtypeStruct((B, S, RET_W), BF16),
        compiler_params=_params(("parallel", "arbitrary")),
        name="retention",
    )(log_g, rq, rk, rv, rg)


def _diff_attn_kernel(sc_ref, q_ref, k_ref, vt_ref, btab_ref, g_ref, ng_ref, o_ref,
                      qz_ref, bias_ref, m_ref, acc_ref, s0_ref, p_ref, *, lam_init):
    h = pl.program_id(1)
    qi = pl.program_id(2)
    T = ATT_TILE
    nk = vt_ref.shape[0]
    lam = sc_ref[0]
    bias_left = sc_ref[1 + h]
    bias_right = sc_ref[1 + N_DIFF_HEADS + h]

    @pl.when(qi == 0)
    def _():
        for d in range(3):
            x = jnp.broadcast_to(btab_ref[d], (T, 2 * T))
            bias_ref[d] = pltpu.roll(x, 0, 1, stride=1, stride_axis=0)[:, :T]

    qt = q_ref[...].astype(F32).T
    half = lax.broadcasted_iota(jnp.int32, qt.shape, 0) < DIFF_DH
    qz_ref[0] = jnp.where(half, qt, 0.0).astype(BF16)
    qz_ref[1] = jnp.where(half, 0.0, qt).astype(BF16)
    m_ref[...] = jnp.full_like(m_ref, M_INIT)
    acc_ref[...] = jnp.zeros_like(acc_ref)
    ones = jnp.ones((ONES_ROWS, T), BF16)

    chains = [(ko, qb, c) for ko in range(ATT_QBLOCKS) for qb in range(ATT_QBLOCKS) for c in range(2)]

    def qcols(qb):
        return slice(qb * T, (qb + 1) * T)

    def scores(chain, kp):
        ko, qb, c = chain
        row0 = pl.multiple_of((kp * ATT_QBLOCKS + ko) * T, T)
        k = k_ref[pl.ds(row0, T), :]
        return jnp.dot(k, qz_ref[c, :, qcols(qb)], preferred_element_type=F32)

    def biased_rows(s, d):
        def rows(r):
            sr = s[r:r + ATT_ROWS]
            return sr if d is None else sr + bias_ref[d, r:r + ATT_ROWS, :]
        return rows

    def running_max(mx, x):
        return x if mx is None else jnp.maximum(mx, x)

    def softmax_pv(chain, vt, rows, mx, next_rows, p_ref):
        _, qb, c = chain
        cols = qcols(qb)
        m_old = m_ref[c, :, cols]
        m_new = jnp.maximum(m_old, jnp.max(mx, axis=0, keepdims=True))
        mx_next = None
        for r in range(0, T, ATT_ROWS):
            p_ref[r:r + ATT_ROWS, :] = jnp.exp2(rows(r) - m_new).astype(BF16)
            if next_rows is not None:
                mx_next = running_max(mx_next, next_rows(r))
        acc_ref[c, :, cols] = (jnp.exp2(m_old - m_new) * acc_ref[c, :, cols]
                               + jnp.dot(vt, p_ref[...], preferred_element_type=F32))
        m_ref[c, :, cols] = m_new
        return mx_next

    def step(kp, offset):
        vts = [jnp.concatenate([vt_ref[kp * ATT_QBLOCKS + ko], ones], axis=0)
               for ko in range(ATT_QBLOCKS)]
        kpn = jnp.minimum(kp + 1, nsteps - 1)

        def block_delta(chain):
            ko, qb, _ = chain
            return None if offset is None else offset * ATT_QBLOCKS + ko - qb

        def rows_of(i):
            delta = block_delta(chains[i])
            return biased_rows(s[i], delta + 1 if delta in (-1, 0, 1) else None)

        s = [s0_ref[i] for i in range(ATT_DEPTH)]
        rows = rows_of(0)
        mx = None
        for r in range(0, T, ATT_ROWS):
            mx = running_max(mx, rows(r))
        for i, chain in enumerate(chains):
            ko, qb, c = chain
            ahead = i + ATT_DEPTH
            if ahead < len(chains):
                s.append(scores(chains[ahead], kp))
            else:
                s0_ref[ahead - len(chains)] = scores(chains[ahead - len(chains)], kpn)
            delta = block_delta(chain)
            if c == 0 and delta == -1:
                m_ref[:, :, qcols(qb)] = m_ref[:, :, qcols(qb)] + bias_left
            if c == 0 and delta == 2:
                m_ref[:, :, qcols(qb)] = m_ref[:, :, qcols(qb)] - bias_right
            next_rows = rows_of(i + 1) if i + 1 < len(chains) else None
            mx = softmax_pv(chain, vts[ko], rows, mx, next_rows, p_ref.at[i % ATT_PBUFS])
            rows = next_rows

    nsteps = nk // ATT_QBLOCKS
    for i in range(ATT_DEPTH):
        s0_ref[i] = scores(chains[i], 0)
    n_left = jnp.maximum(qi - 1, 0)

    def far_pair_left(i, carry):
        step(2 * i, None)
        step(2 * i + 1, None)
        return carry

    lax.fori_loop(0, n_left // 2, far_pair_left, 0)

    @pl.when(n_left % 2 == 1)
    def _():
        step(n_left - 1, None)

    @pl.when(qi >= 1)
    def _():
        step(jnp.maximum(qi - 1, 0), -1)

    @pl.when(qi + 1 < nsteps)
    def _():
        step(qi, 0)
        step(jnp.minimum(qi + 1, nsteps - 1), 1)

    @pl.when(qi + 1 >= nsteps)
    def _():
        step(qi, 0)

    first_right = qi + 2
    n_right = jnp.maximum(nsteps - first_right, 0)

    @pl.when(n_right % 2 == 1)
    def _():
        step(jnp.minimum(first_right, nsteps - 1), None)

    paired_right = first_right + n_right % 2

    def far_pair_right(i, carry):
        step(paired_right + 2 * i, None)
        step(paired_right + 2 * i + 1, None)
        return carry

    lax.fori_loop(0, n_right // 2, far_pair_right, 0)

    def normalised(c):
        return acc_ref[c, :DIFF_DV, :] / acc_ref[c, DIFF_DV:DIFF_DV + 1, :]

    ot = normalised(0) - lam * normalised(1)
    o = _rms(ot.T, ng_ref[...]) * (1.0 - lam_init)
    o_ref[...] = (o * _silu(g_ref[...])).astype(BF16)


def _diff_attn(scal, dq, dk, dvt, btab, dg, ng, lam_init):
    B, S, _ = dq.shape
    T = ATT_TILE
    TQ = ATT_QBLOCKS * T
    q_spec = pl.BlockSpec((None, TQ, DIFF_DV), lambda b, h, qi, sc: (b, qi, h))
    k_spec = pl.BlockSpec((None, S, DIFF_DV), lambda b, h, qi, sc: (b, 0, h))
    vt_spec = pl.BlockSpec((None, S // T, DIFF_DV, T), lambda b, h, qi, sc: (b, 0, h, 0))
    btab_spec = pl.BlockSpec((None, 3, 1, 2 * T), lambda b, h, qi, sc: (h, 0, 0, 0))
    ng_spec = pl.BlockSpec((1, DIFF_DV), lambda b, h, qi, sc: (0, 0))
    grid_spec = pltpu.PrefetchScalarGridSpec(
        num_scalar_prefetch=1,
        grid=(B, N_DIFF_HEADS, S // TQ),
        in_specs=[q_spec, k_spec, vt_spec, btab_spec, q_spec, ng_spec],
        out_specs=q_spec,
        scratch_shapes=[pltpu.VMEM((2, DIFF_DV, TQ), BF16),
                        pltpu.VMEM((3, T, T), F32),
                        pltpu.VMEM((2, 1, TQ), F32),
                        pltpu.VMEM((2, DIFF_DV + ONES_ROWS, TQ), F32),
                        pltpu.VMEM((ATT_DEPTH, T, T), F32),
                        pltpu.VMEM((ATT_PBUFS, T, T), BF16)],
    )
    return pl.pallas_call(
        functools.partial(_diff_attn_kernel, lam_init=lam_init),
        grid_spec=grid_spec,
        out_shape=jax.ShapeDtypeStruct((B, S, DIFF_W), BF16),
        compiler_params=_params(("parallel", "parallel", "arbitrary")),
        name="diff_attn",
    )(scal, dq, dk, dvt, btab, dg, ng)


def _mid_proj_kernel(x_ref, ro_ref, do_ref, wo_ref, g_ref, wi_ref, x1_ref, glu_ref, sg_ref):
    x1 = (x_ref[...]
          + jnp.dot(ro_ref[...], wo_ref[:RET_W, :], preferred_element_type=F32)
          + jnp.dot(do_ref[...], wo_ref[RET_W:, :], preferred_element_type=F32))
    x1_ref[...] = x1
    hb = _rms(x1, g_ref[...]).astype(BF16)
    step = 512
    for c0 in range(0, CONV_W, step):
        a = jnp.dot(hb, wi_ref[:, c0:c0 + step], preferred_element_type=F32)
        b = jnp.dot(hb, wi_ref[:, CONV_W + c0:CONV_W + c0 + step], preferred_element_type=F32)
        gate = jnp.dot(hb, wi_ref[:, 2 * CONV_W + c0:2 * CONV_W + c0 + step], preferred_element_type=F32)
        glu_ref[:, c0:c0 + step] = a * jax.nn.sigmoid(b)
        sg_ref[:, c0:c0 + step] = _silu(gate)


def _mid_proj(x, ro, do, wo, g, wi):
    B, S, D = x.shape
    tm = MID_TILE
    row = lambda n: pl.BlockSpec((None, tm, n), lambda b, i: (b, i, 0))
    return pl.pallas_call(
        _mid_proj_kernel,
        grid=(B, S // tm),
        in_specs=[row(D), row(RET_W), row(DIFF_W), _whole(wo.shape), _whole((1, D)), _whole(wi.shape)],
        out_specs=(row(D), row(CONV_W), row(CONV_W)),
        out_shape=(jax.ShapeDtypeStruct((B, S, D), F32),
                   jax.ShapeDtypeStruct((B, S, CONV_W), F32),
                   jax.ShapeDtypeStruct((B, S, CONV_W), F32)),
        compiler_params=_params(("parallel", "parallel")),
        name="mid_proj",
    )(x, ro, do, wo, g, wi)


def _conv_out_kernel(glu_ref, prev_ref, next_ref, sg_ref, x1_ref, cw_ref, cb_ref, ng_ref, nb_ref,
                     wo_ref, fg_ref, o_ref, win_ref, y_ref):
    i = pl.program_id(1)
    nblk = pl.num_programs(1)
    TS = CONV_TILE
    ncb = CONV_W // LANES
    prev = jnp.where(i > 0, prev_ref[...], 0.0)
    nxt = jnp.where(i < nblk - 1, next_ref[...], 0.0)
    for cb in range(ncb):
        sl = slice(cb * LANES, (cb + 1) * LANES)
        win_ref[cb, 0:HALO, :] = prev[:, sl]
        win_ref[cb, HALO:HALO + TS, :] = glu_ref[:, sl]
        win_ref[cb, HALO + TS:HALO + TS + HALO, :] = nxt[:, sl]

    def chan_block(cb, carry):
        w = cw_ref[cb]
        for r0 in range(0, TS, CONV_ROWS):
            acc = jnp.zeros((CONV_ROWS, LANES), F32)
            for t in range(CONV_K):
                off = r0 + HALO - CONV_PAD + t
                acc = acc + win_ref[cb, pl.ds(off, CONV_ROWS), :] * w[t:t + 1, :]
            y_ref[cb, r0:r0 + CONV_ROWS, :] = acc
        return carry

    lax.fori_loop(0, ncb, chan_block, 0)

    y = jnp.concatenate([y_ref[cb] for cb in range(ncb)], axis=-1) + cb_ref[...]
    mu = jnp.mean(y, axis=-1, keepdims=True)
    yc = y - mu
    var = jnp.mean(yc * yc, axis=-1, keepdims=True)
    yn = yc * lax.rsqrt(var + EPS) * ng_ref[...] + nb_ref[...]
    z = (_silu(yn) * sg_ref[...]).astype(BF16)
    out = x1_ref[...] + jnp.dot(z, wo_ref[...], preferred_element_type=F32)
    o_ref[...] = _rms(out, fg_ref[...])


def _conv_out(glu, sg, x1, cw, cb, ng, nb, wo, fg):
    B, S, D = x1.shape
    TS = CONV_TILE
    hb = TS // HALO
    nh = S // HALO
    row = lambda n: pl.BlockSpec((None, TS, n), lambda b, i: (b, i, 0))
    prev_spec = pl.BlockSpec((None, HALO, CONV_W), lambda b, i: (b, jnp.maximum(i * hb - 1, 0), 0))
    next_spec = pl.BlockSpec((None, HALO, CONV_W), lambda b, i: (b, jnp.minimum((i + 1) * hb, nh - 1), 0))
    ncb = CONV_W // LANES
    return pl.pallas_call(
        _conv_out_kernel,
        grid=(B, S // TS),
        in_specs=[row(CONV_W), prev_spec, next_spec, row(CONV_W), row(D),
                  _whole(cw.shape), _whole((1, CONV_W)), _whole((1, CONV_W)), _whole((1, CONV_W)),
                  _whole(wo.shape), _whole((1, D))],
        out_specs=row(D),
        out_shape=jax.ShapeDtypeStruct((B, S, D), F32),
        scratch_shapes=[pltpu.VMEM((ncb, TS + 2 * HALO, LANES), F32),
                        pltpu.VMEM((ncb, TS, LANES), F32)],
        compiler_params=_params(("parallel", "parallel")),
        name="conv_out",
    )(glu, glu, glu, sg, x1, cw, cb, ng, nb, wo, fg)


def _t5_bucket_np(rel):
    nb = T5_BUCKETS // 2
    max_exact = nb // 2
    ret = (rel > 0).astype(np.int64) * nb
    n = np.abs(rel)
    nf = np.maximum(n, 1).astype(np.float64)
    large = max_exact + (np.log(nf / max_exact) / math.log(T5_MAX_DIST / max_exact)
                         * (nb - max_exact)).astype(np.int64)
    large = np.minimum(large, nb - 1)
    return ret + np.where(n < max_exact, n, large)


def _rope_tables(S):
    half = RET_DK // 2
    inv = 1.0 / (ROPE_BASE ** (jnp.arange(0, RET_DK, 2, dtype=F32) / RET_DK))
    ang = jnp.arange(S).astype(F32)[:, None] * inv[None, :]
    cos = jnp.cos(ang)
    sin = jnp.sin(ang)
    assert cos.shape == (S, half)
    return jnp.concatenate([cos, cos], axis=-1), jnp.concatenate([-sin, sin], axis=-1)


def _trunk(x, p):
    B, S, D = x.shape
    assert D == D_MODEL and ATT_TILE % ROW_TILE == 0
    assert all(S % t == 0 for t in (ATT_TILE * ATT_QBLOCKS, RET_BLOCK, ROW_TILE, MID_TILE, CONV_TILE))

    cos, sin = _rope_tables(S)
    rq, rk, rv, rg, dq, dk, dvt, dg = _even_inproj(x, p["g0"], p["w_in_mix"], cos, sin)
    ro = _retention(p["log_g"], rq, rk, rv, rg)
    do = _diff_attn(p["attn_scal"], dq, dk, dvt, p["btab"], dg, p["diff_norm_g"], p["lam_init"])
    x1, glu, sg = _mid_proj(x, ro, do, p["w_out_mix"], p["g1"], p["w_in_conv"])
    return _conv_out(glu, sg, x1, p["conv_w"], p["conv_b"], p["conv_norm_g"], p["conv_norm_b"],
                     p["w_out_conv"], p["final_g"])


def kernel(x_prompt, x_sample, norm_g, final_g, rel_bias, w_in_mix, ret_decay, lam_q1, lam_k1, lam_q2,
           lam_k2, diff_norm_g, w_out_mix, w_in_conv, conv_w, conv_b, conv_norm_g, conv_norm_b,
           w_out_conv):
    assert norm_g.shape[0] == 2 and w_in_mix.shape[0] == 1 and w_in_conv.shape[0] == 1
    lam_init = 0.8 - 0.6 * math.exp(-0.3 * 0)
    lam = (jnp.exp(jnp.sum(lam_q1[0].astype(F32) * lam_k1[0].astype(F32)))
           - jnp.exp(jnp.sum(lam_q2[0].astype(F32) * lam_k2[0].astype(F32))) + lam_init)
    T = ATT_TILE
    i = np.arange(2 * T)
    idx = np.stack([_t5_bucket_np(np.where(i <= T, off * T - i, off * T + 2 * T - i))
                    for off in (-1, 0, 1)]).astype(np.int32)
    bias2 = rel_bias.astype(F32) * LOG2E
    btab = jnp.take(bias2.T, jnp.asarray(idx), axis=1).reshape(N_DIFF_HEADS, 3, 1, 2 * T)
    far_left = int(_t5_bucket_np(np.array(-T5_MAX_DIST)))
    far_right = int(_t5_bucket_np(np.array(T5_MAX_DIST)))
    attn_scal = jnp.concatenate([lam.reshape(1), bias2[far_left], bias2[far_right]])
    wim = w_in_mix[0]
    ncb = CONV_W // LANES
    p = dict(
        g0=norm_g[0].reshape(1, D_MODEL), g1=norm_g[1].reshape(1, D_MODEL),
        final_g=final_g.reshape(1, D_MODEL),
        w_in_mix=wim.astype(BF16),
        log_g=(-jnp.exp(ret_decay[0].astype(F32))).reshape(-1),
        attn_scal=attn_scal, btab=btab, lam_init=lam_init,
        diff_norm_g=diff_norm_g[0].reshape(1, DIFF_DV),
        w_out_mix=w_out_mix[0].astype(BF16),
        w_in_conv=w_in_conv[0].astype(BF16),
        conv_w=conv_w[0].reshape(CONV_K, ncb, LANES).transpose(1, 0, 2),
        conv_b=conv_b[0].reshape(1, CONV_W),
        conv_norm_g=conv_norm_g[0].reshape(1, CONV_W),
        conv_norm_b=conv_norm_b[0].reshape(1, CONV_W),
        w_out_conv=w_out_conv[0].astype(BF16),
    )
    return (_trunk(x_prompt, p), _trunk(x_sample, p))
```

```python
import functools
import math

import numpy as np
import jax
import jax.numpy as jnp
from jax import lax
from jax.experimental import pallas as pl
from jax.experimental.pallas import tpu as pltpu

F32 = jnp.float32
BF16 = jnp.bfloat16

D_MODEL = 1024
N_RET_HEADS = 4
RET_DK = 128
RET_DV = 256
N_DIFF_HEADS = 8
DIFF_DH = 64
DIFF_DV = 2 * DIFF_DH
RET_W = N_RET_HEADS * RET_DV
DIFF_W = N_DIFF_HEADS * DIFF_DV
CONV_W = 2 * D_MODEL
CONV_K = 31
CONV_PAD = CONV_K // 2
T5_BUCKETS = 32
T5_MAX_DIST = 128
ROPE_BASE = 10000.0
EPS = 1e-6

_C_RQ = 0
_C_RK = _C_RQ + N_RET_HEADS * RET_DK
_C_RV = _C_RK + N_RET_HEADS * RET_DK
_C_RG = _C_RV + RET_W
_C_DQ = _C_RG + RET_W
_C_DK = _C_DQ + DIFF_W
_C_DV = _C_DK + DIFF_W
_C_DG = _C_DV + DIFF_W
EVEN_IN_W = _C_DG + DIFF_W

LANES = 128
HALO = 16
VMEM_LIMIT = 56 * 1024 * 1024

ROW_TILE = 256
MID_TILE = 256
RET_CHUNK = 256
RET_BLOCK = 1024
ATT_TILE = 256
ATT_QBLOCKS = 4
ATT_ROWS = 32
ATT_DEPTH = 4
ATT_PBUFS = 4
ONES_ROWS = 16
LOG2E = math.log2(math.e)
CONV_TILE = 512
CONV_ROWS = 64

M_INIT = -1e30


def _rms(x, g):
    return x * lax.rsqrt(jnp.mean(x * x, axis=-1, keepdims=True) + EPS) * g


def _silu(x):
    return x * jax.nn.sigmoid(x)


def _params(sem):
    return pltpu.CompilerParams(dimension_semantics=sem, vmem_limit_bytes=VMEM_LIMIT)


def _whole(shape):
    nd = len(shape)
    return pl.BlockSpec(shape, lambda *_: (0,) * nd, pipeline_mode=pl.Buffered(1))


def _even_inproj_kernel(x_ref, g_ref, w_ref, cos_ref, sin_ref,
                        rq_ref, rk_ref, rv_ref, rg_ref, dq_ref, dk_ref, dvt_ref, dg_ref):
    hb = _rms(x_ref[...], g_ref[...]).astype(BF16)

    def proj(c0, n):
        return jnp.dot(hb, w_ref[:, c0:c0 + n], preferred_element_type=F32)

    cos = cos_ref[...]
    sin = sin_ref[...]

    def rope(u):
        return u * cos + pltpu.roll(u, RET_DK // 2, 1) * sin

    uq = proj(_C_RQ, N_RET_HEADS * RET_DK)
    uk = proj(_C_RK, N_RET_HEADS * RET_DK)
    for j in range(N_RET_HEADS):
        sl = slice(j * RET_DK, (j + 1) * RET_DK)
        rq_ref[:, sl] = rope(uq[:, sl]).astype(BF16)
        rk_ref[:, sl] = (rope(uk[:, sl]) * (RET_DK ** -0.5)).astype(BF16)
    rv_ref[...] = proj(_C_RV, RET_W).astype(BF16)
    rg_ref[...] = proj(_C_RG, RET_W)
    dq_ref[...] = (proj(_C_DQ, DIFF_W) * (DIFF_DH ** -0.5 * LOG2E)).astype(BF16)
    dk_ref[...] = proj(_C_DK, DIFF_W).astype(BF16)
    dg_ref[...] = proj(_C_DG, DIFF_W)
    dvt_ref[...] = proj(_C_DV, DIFF_W).T.astype(BF16)


def _even_inproj(x, g, w, cos, sin):
    B, S, D = x.shape
    tm = ROW_TILE
    row = lambda n: pl.BlockSpec((None, tm, n), lambda b, i: (b, i, 0))
    tab = pl.BlockSpec((tm, LANES), lambda b, i: (i, 0))
    out_shape = (
        jax.ShapeDtypeStruct((B, S, N_RET_HEADS * RET_DK), BF16),
        jax.ShapeDtypeStruct((B, S, N_RET_HEADS * RET_DK), BF16),
        jax.ShapeDtypeStruct((B, S, RET_W), BF16),
        jax.ShapeDtypeStruct((B, S, RET_W), F32),
        jax.ShapeDtypeStruct((B, S, DIFF_W), BF16),
        jax.ShapeDtypeStruct((B, S, DIFF_W), BF16),
        jax.ShapeDtypeStruct((B, S // ATT_TILE, DIFF_W, ATT_TILE), BF16),
        jax.ShapeDtypeStruct((B, S, DIFF_W), F32),
    )
    per_tile = ATT_TILE // tm
    out_specs = (row(N_RET_HEADS * RET_DK), row(N_RET_HEADS * RET_DK), row(RET_W), row(RET_W),
                 row(DIFF_W), row(DIFF_W),
                 pl.BlockSpec((None, None, DIFF_W, tm), lambda b, i: (b, i // per_tile, 0, i % per_tile)),
                 row(DIFF_W))
    return pl.pallas_call(
        _even_inproj_kernel,
        grid=(B, S // tm),
        in_specs=[row(D), _whole((1, D)), _whole(w.shape), tab, tab],
        out_specs=out_specs,
        out_shape=out_shape,
        compiler_params=_params(("parallel", "parallel")),
        name="even_inproj",
    )(x, g, w, cos, sin)


def _retention_kernel(lg_ref, q_ref, k_ref, v_ref, g_ref, o_ref, lst_ref, st_ref, dmat_ref):
    n = pl.program_id(1)
    nblk = pl.num_programs(1) // 2
    C = RET_CHUNK
    cpb = RET_BLOCK // C
    heads = range(N_RET_HEADS)
    lf = [lg_ref[h] for h in heads]
    lb = [lg_ref[N_RET_HEADS + h] for h in heads]
    pos = lax.broadcasted_iota(jnp.int32, (C, 1), 0).astype(F32)

    def kcols(h):
        return slice(h * RET_DK, (h + 1) * RET_DK)

    def vcols(h):
        return slice(h * RET_DV, (h + 1) * RET_DV)

    def decay(x):
        return jnp.exp(jnp.full((1, RET_DV), x, F32))

    def kt_v(h, rows, w_col):
        kw = (k_ref[rows, kcols(h)].astype(F32) * w_col).T.astype(BF16)
        return jnp.dot(kw, v_ref[rows, vcols(h)], preferred_element_type=F32)

    @pl.when(n == 0)
    def _():
        r = lax.broadcasted_iota(jnp.int32, (C, C), 0)
        c = lax.broadcasted_iota(jnp.int32, (C, C), 1)
        d = (r - c).astype(F32)
        for h in heads:
            dmat_ref[h] = jnp.where(d >= 0, jnp.exp(lf[h] * jnp.maximum(d, 0.0)),
                                    jnp.exp(lb[h] * jnp.maximum(-d, 0.0)))

    @pl.when((n == 0) | (n == nblk))
    def _():
        st_ref[...] = jnp.zeros_like(st_ref)

    @pl.when(n < nblk)
    def _():
        blk = nblk - 1 - n
        for cc in reversed(range(cpb)):
            rows = slice(cc * C, (cc + 1) * C)
            chunk = blk * cpb + cc
            upd = [kt_v(h, rows, jnp.exp(lb[h] * pos)) for h in heads]
            for h in heads:
                lst_ref[chunk, h] = st_ref[h].astype(BF16)
                st_ref[h] = decay(lb[h] * C) * st_ref[h] + upd[h]

    @pl.when(n >= nblk)
    def _():
        blk = n - nblk
        for cc in range(cpb):
            rows = slice(cc * C, (cc + 1) * C)
            chunk = blk * cpb + cc
            q = [q_ref[rows, kcols(h)] for h in heads]
            s = [lax.dot_general(q[h], k_ref[rows, kcols(h)], (((1,), (1,)), ((), ())),
                                 preferred_element_type=F32) for h in heads]
            qf = [(q[h].astype(F32) * jnp.exp(lf[h] * (pos + 1.0))).astype(BF16) for h in heads]
            qb = [(q[h].astype(F32) * jnp.exp(lb[h] * (C - pos))).astype(BF16) for h in heads]
            sd = [(s[h] * dmat_ref[h]).astype(BF16) for h in heads]
            o = [jnp.dot(sd[h], v_ref[rows, vcols(h)], preferred_element_type=F32)
                 + jnp.dot(qf[h], st_ref[h].astype(BF16), preferred_element_type=F32)
                 + jnp.dot(qb[h], lst_ref[chunk, h], preferred_element_type=F32) for h in heads]
            upd = [kt_v(h, rows, jnp.exp(lf[h] * (C - 1.0 - pos))) for h in heads]
            for h in heads:
                st_ref[h] = decay(lf[h] * C) * st_ref[h] + upd[h]
                oh = o[h] * lax.rsqrt(jnp.mean(o[h] * o[h], axis=-1, keepdims=True) + EPS)
                o_ref[rows, vcols(h)] = (oh * _silu(g_ref[rows, vcols(h)])).astype(BF16)


def _retention(log_g, rq, rk, rv, rg):
    B, S, _ = rq.shape
    RB = RET_BLOCK
    NB = S // RB
    KW = N_RET_HEADS * RET_DK

    def both(n):
        return jnp.where(n < NB, NB - 1 - n, n - NB)

    def fwd(n):
        return jnp.maximum(n - NB, 0)

    q_spec = pl.BlockSpec((None, RB, KW), lambda b, n, lg: (b, fwd(n), 0))
    k_spec = pl.BlockSpec((None, RB, KW), lambda b, n, lg: (b, both(n), 0))
    v_spec = pl.BlockSpec((None, RB, RET_W), lambda b, n, lg: (b, both(n), 0))
    fwd_spec = pl.BlockSpec((None, RB, RET_W), lambda b, n, lg: (b, fwd(n), 0))
    grid_spec = pltpu.PrefetchScalarGridSpec(
        num_scalar_prefetch=1,
        grid=(B, 2 * NB),
        in_specs=[q_spec, k_spec, v_spec, fwd_spec],
        out_specs=fwd_spec,
        scratch_shapes=[pltpu.VMEM((S // RET_CHUNK, N_RET_HEADS, RET_DK, RET_DV), BF16),
                        pltpu.VMEM((N_RET_HEADS, RET_DK, RET_DV), F32),
                        pltpu.VMEM((N_RET_HEADS, RET_CHUNK, RET_CHUNK), F32)],
    )
    return pl.pallas_call(
        _retention_kernel,
        grid_spec=grid_spec,
        out_shape=jax.ShapeDtypeStruct((B, S, RET_W), BF16),
        compiler_params=_params(("parallel", "arbitrary")),
        name="retention",
    )(log_g, rq, rk, rv, rg)


def _diff_attn_kernel(sc_ref, q_ref, k_ref, vt_ref, btab_ref, g_ref, ng_ref, o_ref,
                      qz_ref, bias_ref, m_ref, acc_ref, s0_ref, p_ref, *, lam_init):
    h = pl.program_id(1)
    qi = pl.program_id(2)
    T = ATT_TILE
    nk = vt_ref.shape[0]
    lam = sc_ref[0]
    bias_left = sc_ref[1 + h]
    bias_right = sc_ref[1 + N_DIFF_HEADS + h]

    @pl.when(qi == 0)
    def _():
        for d in range(3):
            x = jnp.broadcast_to(btab_ref[d], (T, 2 * T))
            bias_ref[d] = pltpu.roll(x, 0, 1, stride=1, stride_axis=0)[:, :T]
        bias_ref[3] = jnp.zeros((T, T), F32)

    qt = q_ref[...].astype(F32).T
    half = lax.broadcasted_iota(jnp.int32, qt.shape, 0) < DIFF_DH
    qz_ref[0] = jnp.where(half, qt, 0.0).astype(BF16)
    qz_ref[1] = jnp.where(half, 0.0, qt).astype(BF16)
    m_ref[...] = jnp.full_like(m_ref, M_INIT)
    acc_ref[...] = jnp.zeros_like(acc_ref)
    ones = jnp.ones((ONES_ROWS, T), BF16)

    chains = [(ko, qb, c) for ko in range(ATT_QBLOCKS) for qb in range(ATT_QBLOCKS) for c in range(2)]

    def qcols(qb):
        return slice(qb * T, (qb + 1) * T)

    def scores(chain, kp):
        ko, qb, c = chain
        row0 = pl.multiple_of((kp * ATT_QBLOCKS + ko) * T, T)
        k = k_ref[pl.ds(row0, T), :]
        return jnp.dot(k, qz_ref[c, :, qcols(qb)], preferred_element_type=F32)

    def biased_rows(s, d):
        def rows(r):
            return s[r:r + ATT_ROWS] + bias_ref[d, pl.ds(r, ATT_ROWS), :]
        return rows

    def running_max(mx, x):
        return x if mx is None else jnp.maximum(mx, x)

    def softmax_pv(chain, vt, rows, mx, next_rows, p_ref):
        _, qb, c = chain
        cols = qcols(qb)
        m_old = m_ref[c, :, cols]
        m_new = jnp.maximum(m_old, jnp.max(mx, axis=0, keepdims=True))
        mx_next = None
        for r in range(0, T, ATT_ROWS):
            p_ref[r:r + ATT_ROWS, :] = jnp.exp2(rows(r) - m_new).astype(BF16)
            if next_rows is not None:
                mx_next = running_max(mx_next, next_rows(r))
        acc_ref[c, :, cols] = (jnp.exp2(m_old - m_new) * acc_ref[c, :, cols]
                               + jnp.dot(vt, p_ref[...], preferred_element_type=F32))
        m_ref[c, :, cols] = m_new
        return mx_next

    def step(kp):
        vts = [jnp.concatenate([vt_ref[kp * ATT_QBLOCKS + ko], ones], axis=0)
               for ko in range(ATT_QBLOCKS)]
        kpn = jnp.minimum(kp + 1, nsteps - 1)

        def block_delta(chain):
            ko, qb, _ = chain
            return (kp - qi) * ATT_QBLOCKS + ko - qb

        def rows_of(i):
            delta = block_delta(chains[i])
            return biased_rows(s[i], jnp.where(jnp.abs(delta) <= 1, delta + 1, 3))

        s = [s0_ref[i] for i in range(ATT_DEPTH)]
        rows = rows_of(0)
        mx = None
        for r in range(0, T, ATT_ROWS):
            mx = running_max(mx, rows(r))
        for i, chain in enumerate(chains):
            ko, qb, c = chain
            ahead = i + ATT_DEPTH
            if ahead < len(chains):
                s.append(scores(chains[ahead], kp))
            else:
                s0_ref[ahead - len(chains)] = scores(chains[ahead - len(chains)], kpn)
            if c == 0:
                delta = block_delta(chain)
                shift = (jnp.where(delta == -1, bias_left, 0.0) - jnp.where(delta == 2, bias_right, 0.0))
                m_ref[:, :, qcols(qb)] = m_ref[:, :, qcols(qb)] + shift
            next_rows = rows_of(i + 1) if i + 1 < len(chains) else None
            mx = softmax_pv(chain, vts[ko], rows, mx, next_rows, p_ref.at[i % ATT_PBUFS])
            rows = next_rows

    nsteps = nk // ATT_QBLOCKS
    for i in range(ATT_DEPTH):
        s0_ref[i] = scores(chains[i], 0)

    def step_pair(i, carry):
        step(2 * i)
        step(2 * i + 1)
        return carry

    lax.fori_loop(0, nsteps // 2, step_pair, 0)

    def normalised(c):
        return acc_ref[c, :DIFF_DV, :] / acc_ref[c, DIFF_DV:DIFF_DV + 1, :]

    ot = normalised(0) - lam * normalised(1)
    o = _rms(ot.T, ng_ref[...]) * (1.0 - lam_init)
    o_ref[...] = (o * _silu(g_ref[...])).astype(BF16)


def _diff_attn(scal, dq, dk, dvt, btab, dg, ng, lam_init):
    B, S, _ = dq.shape
    T = ATT_TILE
    TQ = ATT_QBLOCKS * T
    q_spec = pl.BlockSpec((None, TQ, DIFF_DV), lambda b, h, qi, sc: (b, qi, h))
    k_spec = pl.BlockSpec((None, S, DIFF_DV), lambda b, h, qi, sc: (b, 0, h))
    vt_spec = pl.BlockSpec((None, S // T, DIFF_DV, T), lambda b, h, qi, sc: (b, 0, h, 0))
    btab_spec = pl.BlockSpec((None, 3, 1, 2 * T), lambda b, h, qi, sc: (h, 0, 0, 0))
    ng_spec = pl.BlockSpec((1, DIFF_DV), lambda b, h, qi, sc: (0, 0))
    grid_spec = pltpu.PrefetchScalarGridSpec(
        num_scalar_prefetch=1,
        grid=(B, N_DIFF_HEADS, S // TQ),
        in_specs=[q_spec, k_spec, vt_spec, btab_spec, q_spec, ng_spec],
        out_specs=q_spec,
        scratch_shapes=[pltpu.VMEM((2, DIFF_DV, TQ), BF16),
                        pltpu.VMEM((4, T, T), F32),
                        pltpu.VMEM((2, 1, TQ), F32),
                        pltpu.VMEM((2, DIFF_DV + ONES_ROWS, TQ), F32),
                        pltpu.VMEM((ATT_DEPTH, T, T), F32),
                        pltpu.VMEM((ATT_PBUFS, T, T), BF16)],
    )
    return pl.pallas_call(
        functools.partial(_diff_attn_kernel, lam_init=lam_init),
        grid_spec=grid_spec,
        out_shape=jax.ShapeDtypeStruct((B, S, DIFF_W), BF16),
        compiler_params=_params(("parallel", "parallel", "arbitrary")),
        name="diff_attn",
    )(scal, dq, dk, dvt, btab, dg, ng)


def _mid_proj_kernel(x_ref, ro_ref, do_ref, wo_ref, g_ref, wi_ref, x1_ref, glu_ref, sg_ref):
    x1 = (x_ref[...]
          + jnp.dot(ro_ref[...], wo_ref[:RET_W, :], preferred_element_type=F32)
          + jnp.dot(do_ref[...], wo_ref[RET_W:, :], preferred_element_type=F32))
    x1_ref[...] = x1
    hb = _rms(x1, g_ref[...]).astype(BF16)
    step = 512
    for c0 in range(0, CONV_W, step):
        a = jnp.dot(hb, wi_ref[:, c0:c0 + step], preferred_element_type=F32)
        b = jnp.dot(hb, wi_ref[:, CONV_W + c0:CONV_W + c0 + step], preferred_element_type=F32)
        gate = jnp.dot(hb, wi_ref[:, 2 * CONV_W + c0:2 * CONV_W + c0 + step], preferred_element_type=F32)
        glu_ref[:, c0:c0 + step] = a * jax.nn.sigmoid(b)
        sg_ref[:, c0:c0 + step] = _silu(gate)


def _mid_proj(x, ro, do, wo, g, wi):
    B, S, D = x.shape
    tm = MID_TILE
    row = lambda n: pl.BlockSpec((None, tm, n), lambda b, i: (b, i, 0))
    return pl.pallas_call(
        _mid_proj_kernel,
        grid=(B, S // tm),
        in_specs=[row(D), row(RET_W), row(DIFF_W), _whole(wo.shape), _whole((1, D)), _whole(wi.shape)],
        out_specs=(row(D), row(CONV_W), row(CONV_W)),
        out_shape=(jax.ShapeDtypeStruct((B, S, D), F32),
                   jax.ShapeDtypeStruct((B, S, CONV_W), F32),
                   jax.ShapeDtypeStruct((B, S, CONV_W), F32)),
        compiler_params=_params(("parallel", "parallel")),
        name="mid_proj",
    )(x, ro, do, wo, g, wi)


def _conv_out_kernel(glu_ref, prev_ref, next_ref, sg_ref, x1_ref, cw_ref, cb_ref, ng_ref, nb_ref,
                     wo_ref, fg_ref, o_ref, win_ref, y_ref):
    i = pl.program_id(1)
    nblk = pl.num_programs(1)
    TS = CONV_TILE
    ncb = CONV_W // LANES
    prev = jnp.where(i > 0, prev_ref[...], 0.0)
    nxt = jnp.where(i < nblk - 1, next_ref[...], 0.0)
    for cb in range(ncb):
        sl = slice(cb * LANES, (cb + 1) * LANES)
        win_ref[cb, 0:HALO, :] = prev[:, sl]
        win_ref[cb, HALO:HALO + TS, :] = glu_ref[:, sl]
        win_ref[cb, HALO + TS:HALO + TS + HALO, :] = nxt[:, sl]

    def chan_block(cb, carry):
        w = cw_ref[cb]
        for r0 in range(0, TS, CONV_ROWS):
            acc = jnp.zeros((CONV_ROWS, LANES), F32)
            for t in range(CONV_K):
                off = r0 + HALO - CONV_PAD + t
                acc = acc + win_ref[cb, pl.ds(off, CONV_ROWS), :] * w[t:t + 1, :]
            y_ref[cb, r0:r0 + CONV_ROWS, :] = acc
        return carry

    lax.fori_loop(0, ncb, chan_block, 0)

    y = jnp.concatenate([y_ref[cb] for cb in range(ncb)], axis=-1) + cb_ref[...]
    mu = jnp.mean(y, axis=-1, keepdims=True)
    yc = y - mu
    var = jnp.mean(yc * yc, axis=-1, keepdims=True)
    yn = yc * lax.rsqrt(var + EPS) * ng_ref[...] + nb_ref[...]
    z = (_silu(yn) * sg_ref[...]).astype(BF16)
    out = x1_ref[...] + jnp.dot(z, wo_ref[...], preferred_element_type=F32)
    o_ref[...] = _rms(out, fg_ref[...])


def _conv_out(glu, sg, x1, cw, cb, ng, nb, wo, fg):
    B, S, D = x1.shape
    TS = CONV_TILE
    hb = TS // HALO
    nh = S // HALO
    row = lambda n: pl.BlockSpec((None, TS, n), lambda b, i: (b, i, 0))
    prev_spec = pl.BlockSpec((None, HALO, CONV_W), lambda b, i: (b, jnp.maximum(i * hb - 1, 0), 0))
    next_spec = pl.BlockSpec((None, HALO, CONV_W), lambda b, i: (b, jnp.minimum((i + 1) * hb, nh - 1), 0))
    ncb = CONV_W // LANES
    return pl.pallas_call(
        _conv_out_kernel,
        grid=(B, S // TS),
        in_specs=[row(CONV_W), prev_spec, next_spec, row(CONV_W), row(D),
                  _whole(cw.shape), _whole((1, CONV_W)), _whole((1, CONV_W)), _whole((1, CONV_W)),
                  _whole(wo.shape), _whole((1, D))],
        out_specs=row(D),
        out_shape=jax.ShapeDtypeStruct((B, S, D), F32),
        scratch_shapes=[pltpu.VMEM((ncb, TS + 2 * HALO, LANES), F32),
                        pltpu.VMEM((ncb, TS, LANES), F32)],
        compiler_params=_params(("parallel", "parallel")),
        name="conv_out",
    )(glu, glu, glu, sg, x1, cw, cb, ng, nb, wo, fg)


def _t5_bucket_np(rel):
    nb = T5_BUCKETS // 2
    max_exact = nb // 2
    ret = (rel > 0).astype(np.int64) * nb
    n = np.abs(rel)
    nf = np.maximum(n, 1).astype(np.float64)
    large = max_exact + (np.log(nf / max_exact) / math.log(T5_MAX_DIST / max_exact)
                         * (nb - max_exact)).astype(np.int64)
    large = np.minimum(large, nb - 1)
    return ret + np.where(n < max_exact, n, large)


def _rope_tables(S):
    half = RET_DK // 2
    inv = 1.0 / (ROPE_BASE ** (jnp.arange(0, RET_DK, 2, dtype=F32) / RET_DK))
    ang = jnp.arange(S).astype(F32)[:, None] * inv[None, :]
    cos = jnp.cos(ang)
    sin = jnp.sin(ang)
    assert cos.shape == (S, half)
    return jnp.concatenate([cos, cos], axis=-1), jnp.concatenate([-sin, sin], axis=-1)


def _trunk(x, p):
    B, S, D = x.shape
    assert D == D_MODEL and ATT_TILE % ROW_TILE == 0
    assert all(S % t == 0 for t in (2 * ATT_TILE * ATT_QBLOCKS, RET_BLOCK, ROW_TILE, MID_TILE, CONV_TILE))

    cos, sin = _rope_tables(S)
    rq, rk, rv, rg, dq, dk, dvt, dg = _even_inproj(x, p["g0"], p["w_in_mix"], cos, sin)
    ro = _retention(p["log_g"], rq, rk, rv, rg)
    do = _diff_attn(p["attn_scal"], dq, dk, dvt, p["btab"], dg, p["diff_norm_g"], p["lam_init"])
    x1, glu, sg = _mid_proj(x, ro, do, p["w_out_mix"], p["g1"], p["w_in_conv"])
    return _conv_out(glu, sg, x1, p["conv_w"], p["conv_b"], p["conv_norm_g"], p["conv_norm_b"],
                     p["w_out_conv"], p["final_g"])


def kernel(x_prompt, x_sample, norm_g, final_g, rel_bias, w_in_mix, ret_decay, lam_q1, lam_k1, lam_q2,
           lam_k2, diff_norm_g, w_out_mix, w_in_conv, conv_w, conv_b, conv_norm_g, conv_norm_b,
           w_out_conv):
    assert norm_g.shape[0] == 2 and w_in_mix.shape[0] == 1 and w_in_conv.shape[0] == 1
    lam_init = 0.8 - 0.6 * math.exp(-0.3 * 0)
    lam = (jnp.exp(jnp.sum(lam_q1[0].astype(F32) * lam_k1[0].astype(F32)))
           - jnp.exp(jnp.sum(lam_q2[0].astype(F32) * lam_k2[0].astype(F32))) + lam_init)
    T = ATT_TILE
    i = np.arange(2 * T)
    idx = np.stack([_t5_bucket_np(np.where(i <= T, off * T - i, off * T + 2 * T - i))
                    for off in (-1, 0, 1)]).astype(np.int32)
    bias2 = rel_bias.astype(F32) * LOG2E
    btab = jnp.take(bias2.T, jnp.asarray(idx), axis=1).reshape(N_DIFF_HEADS, 3, 1, 2 * T)
    far_left = int(_t5_bucket_np(np.array(-T5_MAX_DIST)))
    far_right = int(_t5_bucket_np(np.array(T5_MAX_DIST)))
    attn_scal = jnp.concatenate([lam.reshape(1), bias2[far_left], bias2[far_right]])
    wim = w_in_mix[0]
    ncb = CONV_W // LANES
    p = dict(
        g0=norm_g[0].reshape(1, D_MODEL), g1=norm_g[1].reshape(1, D_MODEL),
        final_g=final_g.reshape(1, D_MODEL),
        w_in_mix=wim.astype(BF16),
        log_g=(-jnp.exp(ret_decay[0].astype(F32))).reshape(-1),
        attn_scal=attn_scal, btab=btab, lam_init=lam_init,
        diff_norm_g=diff_norm_g[0].reshape(1, DIFF_DV),
        w_out_mix=w_out_mix[0].astype(BF16),
        w_in_conv=w_in_conv[0].astype(BF16),
        conv_w=conv_w[0].reshape(CONV_K, ncb, LANES).transpose(1, 0, 2),
        conv_b=conv_b[0].reshape(1, CONV_W),
        conv_norm_g=conv_norm_g[0].reshape(1, CONV_W),
        conv_norm_b=conv_norm_b[0].reshape(1, CONV_W),
        w_out_conv=w_out_conv[0].astype(BF16),
    )
    return (_trunk(x_prompt, p), _trunk(x_sample, p))
```
